```python
import math
import jax, jax.numpy as jnp
from jax import lax
import numpy as np

D_MODEL = 2048
BATCH = 4
SEQ = 8192
DEPTH = 4

GRID_W = 64
CTX_LEN = 256
HEAD_DIM = 128
N_Q_HEADS = 8
N_KV_HEADS = 2
Q_PER_KV = N_Q_HEADS // N_KV_HEADS
ATTN_WIDTH = N_Q_HEADS * HEAD_DIM
KV_WIDTH = N_KV_HEADS * HEAD_DIM
WINDOW = 128
BLOCK = 128
ROPE_THETA = 10000.0
NEG_INF = -1e30
CONV_WIDTH = D_MODEL - ATTN_WIDTH
CONV_K = 3
POOL_WINDOWS = (2, 4, 8, 16)
POOL_GROUP = D_MODEL // 8
POOL_WIDTH = len(POOL_WINDOWS) * POOL_GROUP
FOURIER_HEADS = 4
FOURIER_GROUP = (D_MODEL - POOL_WIDTH) // FOURIER_HEADS
IN_AB_SPLITS = (ATTN_WIDTH, KV_WIDTH, KV_WIDTH, CONV_WIDTH, CONV_WIDTH, CONV_WIDTH)
IN_AB_WIDTH = sum(IN_AB_SPLITS)
IN_CD_WIDTH = POOL_WIDTH + FOURIER_HEADS * FOURIER_GROUP
D_FF = 5632
N_EXPERTS = 8
TOP_K = 2
D_FF_EXPERT = 5632
N_MOD = 6
LN_EPS = 1e-5

kernel_name = "hybrid_swa_conv_pool_fourier_moe_dit"

F32 = jnp.float32


def layer_norm(x, g, b):
    xf = x.astype(F32)
    mu = jnp.mean(xf, -1, keepdims=True)
    var = jnp.mean(jnp.square(xf - mu), -1, keepdims=True)
    return ((xf - mu) * lax.rsqrt(var + LN_EPS)).astype(x.dtype) * g + b


def modulation(cond, w, b):
    m = jax.nn.silu(cond) @ w + b
    return [t[:, None, :] for t in jnp.split(m, N_MOD, axis=-1)]


def axial_rope_tables(n, dtype):
    rows = n // GRID_W
    row = jnp.repeat(jnp.arange(rows, dtype=F32), GRID_W)
    col = jnp.tile(jnp.arange(GRID_W, dtype=F32), rows)
    half = HEAD_DIM // 2
    inv = ROPE_THETA ** (-jnp.arange(0, half, 2, dtype=F32) / half)
    ang_r = row[:, None] * inv
    ang_c = col[:, None] * inv
    ang = jnp.concatenate([ang_r, ang_r, ang_c, ang_c], -1)
    return jnp.cos(ang).astype(dtype)[:, None, :], jnp.sin(ang).astype(dtype)[:, None, :]


def apply_rope(u, cos, sin):
    r1, r2, c1, c2 = jnp.split(u, 4, axis=-1)
    rot = jnp.concatenate([-r2, r1, -c2, c1], -1)
    return u * cos + rot * sin


def to_heads(t, nh):
    return t.reshape(t.shape[0], t.shape[1], nh, HEAD_DIM)


def band_attention(q, k, v, kc, vc, sink):
    bsz, n = q.shape[0], q.shape[1]
    nb = n // BLOCK
    scale = HEAD_DIM ** -0.5
    qb = q.reshape(bsz, nb, BLOCK, N_KV_HEADS, Q_PER_KV, HEAD_DIM)

    def band(t):
        tp = jnp.pad(t, ((0, 0), (BLOCK, BLOCK), (0, 0), (0, 0)))
        tp = tp.reshape(bsz, nb + 2, BLOCK, N_KV_HEADS, HEAD_DIM)
        return jnp.concatenate([tp[:, :-2], tp[:, 1:-1], tp[:, 2:]], axis=2)

    kb, vb = band(k), band(v)
    s_loc = jnp.einsum('bnqhgd,bnkhd->bnhgqk', qb, kb, preferred_element_type=F32) * scale
    s_ctx = jnp.einsum('bnqhgd,bkhd->bnhgqk', qb, kc, preferred_element_type=F32) * scale
    qi = jnp.arange(BLOCK)[:, None]
    kj = jnp.arange(3 * BLOCK)[None, :]
    rel = kj - BLOCK - qi
    kpos = jnp.arange(nb)[:, None, None] * BLOCK + kj[None] - BLOCK
    valid = (jnp.abs(rel) <= WINDOW)[None] & (kpos >= 0) & (kpos < n)
    s_loc = jnp.where(valid[None, :, None, None], s_loc, NEG_INF)
    s_sink = jnp.broadcast_to(sink.astype(F32).reshape(1, 1, N_KV_HEADS, Q_PER_KV, 1, 1),
                              s_ctx.shape[:-1] + (1,))
    p = jax.nn.softmax(jnp.concatenate([s_sink, s_ctx, s_loc], -1), axis=-1).astype(v.dtype)
    n_ctx = kc.shape[1]
    p_ctx = p[..., 1:1 + n_ctx]
    p_loc = p[..., 1 + n_ctx:]
    o = (jnp.einsum('bnhgqk,bnkhd->bnqhgd', p_loc, vb)
         + jnp.einsum('bnhgqk,bkhd->bnqhgd', p_ctx, vc))
    return o.reshape(bsz, n, ATTN_WIDTH)


def context_attention(qc, kc, vc, sink):
    bsz, n_ctx = qc.shape[0], qc.shape[1]
    qg = qc.reshape(bsz, n_ctx, N_KV_HEADS, Q_PER_KV, HEAD_DIM)
    s = jnp.einsum('blhgd,bkhd->bhglk', qg, kc, preferred_element_type=F32) * (HEAD_DIM ** -0.5)
    s_sink = jnp.broadcast_to(sink.astype(F32).reshape(1, N_KV_HEADS, Q_PER_KV, 1, 1), s.shape[:-1] + (1,))
    p = jax.nn.softmax(jnp.concatenate([s_sink, s], -1), axis=-1)[..., 1:].astype(vc.dtype)
    o = jnp.einsum('bhglk,bkhd->blhgd', p, vc)
    return o.reshape(bsz, n_ctx, ATTN_WIDTH)


def split_ab(p):
    idx = [int(i) for i in np.cumsum(IN_AB_SPLITS)[:-1]]
    return jnp.split(p, idx, axis=-1)


def gated_short_conv(u, gb, gc, w):
    z = gc * u
    zp = jnp.pad(z, ((0, 0), (1, 1), (0, 0)))
    y = zp[:, :-2] * w[0] + zp[:, 1:-1] * w[1] + zp[:, 2:] * w[2]
    return gb * y


def even_mixer(h, hc, w_in, conv_w, sink, w_out, cos, sin, ctx_out):
    q, k, v, u, gb, gc = split_ab(h @ w_in)
    qc, kc, vc, uc, gbc, gcc = split_ab(hc @ w_in)
    q = apply_rope(to_heads(q, N_Q_HEADS), cos, sin)
    k = apply_rope(to_heads(k, N_KV_HEADS), cos, sin)
    kc = to_heads(kc, N_KV_HEADS)
    vc = to_heads(vc, N_KV_HEADS)
    a = band_attention(q, k, to_heads(v, N_KV_HEADS), kc, vc, sink)
    s = gated_short_conv(u, gb, gc, conv_w)
    y = jnp.concatenate([a, s], -1) @ w_out
    if not ctx_out:
        return y, None
    ac = context_attention(to_heads(qc, N_Q_HEADS), kc, vc, sink)
    sc = gated_short_conv(uc, gbc, gcc, conv_w)
    yc = jnp.concatenate([ac, sc], -1) @ w_out
    return y, yc


def multiscale_pool(u):
    n = u.shape[1]
    t = jnp.arange(n)
    uf = u.astype(F32)
    cs = jnp.pad(jnp.cumsum(uf, axis=1), ((0, 0), (1, 0), (0, 0)))
    outs = []
    for gi, w in enumerate(POOL_WINDOWS):
        lo = jnp.clip(t - w // 2, 0, n - 1)
        hi = jnp.clip(t + (w - w // 2) - 1, 0, n - 1)
        seg = cs[:, :, gi * POOL_GROUP:(gi + 1) * POOL_GROUP]
        cnt = (hi - lo + 1).astype(F32)[:, None]
        mean = (seg[:, hi + 1] - seg[:, lo]) / cnt
        outs.append(mean - uf[..., gi * POOL_GROUP:(gi + 1) * POOL_GROUP])
    return jnp.concatenate(outs, -1).astype(u.dtype)


def odd_mixer(h, w_in, pool_w, pool_scale, fourier_w, w_out):
    bsz, n, _ = h.shape
    p = h @ w_in
    u_pool, u_four = p[..., :POOL_WIDTH], p[..., POOL_WIDTH:]
    pooled = multiscale_pool(u_pool).reshape(bsz, n, len(POOL_WINDOWS), POOL_GROUP)
    yp = jnp.einsum('bngc,gcd->bngd', pooled, pool_w).reshape(bsz, n, POOL_WIDTH) * pool_scale
    uf = u_four.reshape(bsz, n, FOURIER_HEADS, FOURIER_GROUP).astype(F32)
    f = jnp.fft.fft2(uf, axes=(1, 3), norm='ortho').real.astype(h.dtype)
    yf = jnp.einsum('bngc,gcd->bngd', f, fourier_w).reshape(bsz, n, FOURIER_HEADS * FOURIER_GROUP)
    return jnp.concatenate([yp, yf], -1) @ w_out


def swiglu(h, wg, wu, wd):
    return (jax.nn.silu(h @ wg) * (h @ wu)) @ wd


def moe_swiglu(h, router_w, wg, wu, wd):
    logits = jnp.einsum('bnd,de->bne', h, router_w, preferred_element_type=F32)
    top_v, top_i = lax.top_k(logits, TOP_K)
    gates = jax.nn.softmax(top_v, axis=-1)
    combine = jnp.sum(jax.nn.one_hot(top_i, N_EXPERTS, dtype=F32) * gates[..., None], axis=-2).astype(h.dtype)
    y = jnp.zeros_like(h)
    for e in range(N_EXPERTS):
        y = y + combine[..., e:e + 1] * swiglu(h, wg[e], wu[e], wd[e])
    return y


def setup_inputs(seed: int = 0) -> dict:
    key = jax.random.key(seed)
    ks = jax.random.split(key, 24)
    n_even, n_odd = (DEPTH + 1) // 2, DEPTH // 2
    beta = (8.0 * DEPTH) ** -0.25
    D = D_MODEL

    def nrm(k, shape, s):
        return jax.random.normal(k, shape, F32) * s

    return {
        "x": nrm(ks[0], (BATCH, SEQ, D), 1.0),
        "c": nrm(ks[1], (BATCH, D), 1.0),
        "ctx": nrm(ks[2], (BATCH, CTX_LEN, D), 1.0),
        "c_ctx": nrm(ks[3], (D,), 1.0),
        "w_mod": nrm(ks[4], (DEPTH, D, N_MOD * D), 0.5 * D ** -0.5),
        "b_mod": nrm(ks[5], (DEPTH, N_MOD * D), 0.01),
        "w_mix_out": nrm(ks[6], (DEPTH, D, D), beta * D ** -0.5),
        "ln_g": 1.0 + nrm(ks[7], (DEPTH, 2, D), 0.01),
        "ln_b": nrm(ks[8], (DEPTH, 2, D), 0.01),
        "w_in_ab": nrm(ks[9], (n_even, D, IN_AB_WIDTH), D ** -0.5),
        "conv_w": nrm(ks[10], (n_even, CONV_K, CONV_WIDTH), CONV_K ** -0.5),
        "attn_sink": nrm(ks[11], (n_even, N_Q_HEADS), 1.0),
        "w_in_cd": nrm(ks[12], (n_odd, D, IN_CD_WIDTH), D ** -0.5),
        "pool_w": nrm(ks[13], (n_odd, len(POOL_WINDOWS), POOL_GROUP, POOL_GROUP), POOL_GROUP ** -0.5),
        "pool_scale": 1.0 + nrm(ks[14], (n_odd, POOL_WIDTH), 0.02),
        "fourier_w": nrm(ks[15], (n_odd, FOURIER_HEADS, FOURIER_GROUP, FOURIER_GROUP), FOURIER_GROUP ** -0.5),
        "ffn_w_gate": nrm(ks[16], (n_even, D, D_FF), D ** -0.5),
        "ffn_w_up": nrm(ks[17], (n_even, D, D_FF), D ** -0.5),
        "ffn_w_down": nrm(ks[18], (n_even, D_FF, D), beta * D_FF ** -0.5),
        "router_w": nrm(ks[19], (n_odd, D, N_EXPERTS), D ** -0.5),
        "moe_w_gate": nrm(ks[20], (n_odd, N_EXPERTS, D, D_FF_EXPERT), D ** -0.5),
        "moe_w_up": nrm(ks[21], (n_odd, N_EXPERTS, D, D_FF_EXPERT), D ** -0.5),
        "moe_w_down": nrm(ks[22], (n_odd, N_EXPERTS, D_FF_EXPERT, D), beta * D_FF_EXPERT ** -0.5),
    }


def reference(x, c, ctx, c_ctx, w_mod, b_mod, w_mix_out, ln_g, ln_b, w_in_ab, conv_w, attn_sink,
              w_in_cd, pool_w, pool_scale, fourier_w, ffn_w_gate, ffn_w_up, ffn_w_down,
              router_w, moe_w_gate, moe_w_up, moe_w_down):
    alpha = (2.0 * DEPTH) ** 0.25
    n = x.shape[1]
    n_ctx = ctx.shape[1]
    cos, sin = axial_rope_tables(n, x.dtype)
    for l in range(DEPTH):
        even = l % 2 == 0
        j = l // 2
        ctx_out = any(m % 2 == 0 for m in range(l + 1, DEPTH))
        need_ctx = even or ctx_out
        sh1, sc1, g1, sh2, sc2, g2 = modulation(c, w_mod[l], b_mod[l])
        h = x * (1 + sc1) + sh1
        hc = None
        if need_ctx:
            csh1, csc1, cg1, csh2, csc2, cg2 = modulation(c_ctx[None], w_mod[l], b_mod[l])
            hc = ctx * (1 + csc1) + csh1
        if even:
            y, yc = even_mixer(h, hc, w_in_ab[j], conv_w[j], attn_sink[j], w_mix_out[l], cos, sin, ctx_out)
        else:
            y = odd_mixer(h, w_in_cd[j], pool_w[j], pool_scale[j], fourier_w[j], w_mix_out[l])
            yc = odd_mixer(hc, w_in_cd[j], pool_w[j], pool_scale[j], fourier_w[j], w_mix_out[l]) if ctx_out else None
        x = layer_norm(alpha * x + g1 * y, ln_g[l, 0], ln_b[l, 0])
        if ctx_out:
            ctx = layer_norm(alpha * ctx + cg1 * yc, ln_g[l, 0], ln_b[l, 0])
        h = x * (1 + sc2) + sh2
        hh = jnp.concatenate([ctx * (1 + csc2) + csh2, h], axis=1) if ctx_out else h
        if even:
            ff = swiglu(hh, ffn_w_gate[j], ffn_w_up[j], ffn_w_down[j])
        else:
            ff = moe_swiglu(hh, router_w[j], moe_w_gate[j], moe_w_up[j], moe_w_down[j])
        if ctx_out:
            ffc, ff = ff[:, :n_ctx], ff[:, n_ctx:]
            ctx = layer_norm(alpha * ctx + cg2 * ffc, ln_g[l, 1], ln_b[l, 1])
        x = layer_norm(alpha * x + g2 * ff, ln_g[l, 1], ln_b[l, 1])
    return x
```

```python
import functools
from typing import NamedTuple

import numpy as np
import jax
import jax.numpy as jnp
from jax import lax
from jax.experimental import pallas as pl
from jax.experimental.pallas import tpu as pltpu

F32 = jnp.float32
BF16 = jnp.bfloat16
I32 = jnp.int32

HEAD_DIM = 128
N_Q_HEADS = 8
N_KV_HEADS = 2
Q_PER_KV = N_Q_HEADS // N_KV_HEADS
ATTN_WIDTH = N_Q_HEADS * HEAD_DIM
KV_WIDTH = N_KV_HEADS * HEAD_DIM
ATTN_BLOCK = 128
GRID_W = 64
ROPE_THETA = 10000.0
NEG_INF = -1e30
POOL_WINDOWS = (2, 4, 8, 16)
POOL_HALO = 8
HALO_ROWS = 16
FOURIER_HEADS = 4
N_MOD = 6
LN_EPS = 1e-5
FFT_RADIX = 8
FFT_LANES = 256
SEQ_TILE = 256
ROUTE_LANES = 128
EXPERT_TILE = 512
GATHER_ROWS = 256
V7X_VMEM_LIMIT = 56 * 1024 * 1024


class Dims(NamedTuple):
    bsz: int
    n: int
    n_ctx: int
    d: int

    @property
    def r_lat(self):
        return self.bsz * self.n

    @property
    def r_ctx(self):
        return self.bsz * self.n_ctx

    @property
    def r_all(self):
        return self.r_lat + self.r_ctx


def _pick_tile(pref, *dims):
    t = pref
    while any(d % t for d in dims):
        t //= 2
    return t


def _cparams(*sem, vmem=V7X_VMEM_LIMIT):
    return pltpu.CompilerParams(dimension_semantics=sem, vmem_limit_bytes=vmem)


def _residual_ln(x, y, gate, g, b, alpha):
    v = alpha * x + gate * y
    mu = jnp.mean(v, axis=-1, keepdims=True)
    vc = v - mu
    var = jnp.mean(vc * vc, axis=-1, keepdims=True)
    return vc * lax.rsqrt(var + LN_EPS) * g + b


def _silu(t):
    return t * jax.nn.sigmoid(t)


def _seq_position(row, dims):
    lat = row < dims.r_lat
    pos = jnp.where(lat, lax.rem(row, dims.n), lax.rem(row - dims.r_lat, dims.n_ctx))
    length = jnp.where(lat, dims.n, dims.n_ctx)
    return pos, length


def _mod_kernel(cond_ref, w_ref, b_ref, o_ref):
    s = _silu(cond_ref[...]).astype(BF16)
    o_ref[...] = jnp.dot(s, w_ref[...].astype(BF16), preferred_element_type=F32) + b_ref[...]


def _modulations(c, c_ctx, w_mod, b_mod):
    depth, d, nmod = w_mod.shape
    bsz = c.shape[0]
    rows = -(-(bsz + 1) // 8) * 8
    cond = jnp.zeros((rows, d), F32).at[:bsz].set(c).at[bsz].set(c_ctx)
    tn = _pick_tile(1024, nmod)
    out = pl.pallas_call(
        _mod_kernel,
        grid=(depth, nmod // tn),
        in_specs=[pl.BlockSpec((rows, d), lambda l, j: (0, 0)),
                  pl.BlockSpec((None, d, tn), lambda l, j: (l, 0, j)),
                  pl.BlockSpec((None, 1, tn), lambda l, j: (l, 0, j))],
        out_specs=pl.BlockSpec((None, rows, tn), lambda l, j: (l, 0, j)),
        out_shape=jax.ShapeDtypeStruct((depth, rows, nmod), F32),
        compiler_params=_cparams("arbitrary", "arbitrary"),
        name="modulation",
    )(cond, w_mod, b_mod.reshape(depth, 1, nmod))
    return out.reshape(depth, rows, N_MOD, d)


def _mod_spec(dims, tm):
    return pl.BlockSpec((1, N_MOD, dims.d), lambda i, *_: ((i * tm) // dims.n, 0, 0))


def _row_spec(tm, width):
    return pl.BlockSpec((tm, width), lambda i, *_: (i, 0))


def _const_spec(shape):
    zeros = (0,) * len(shape)
    return pl.BlockSpec(shape, lambda i, *_: zeros)


def _rope_tables(n, tm):
    rows = n // GRID_W
    row = jnp.repeat(jnp.arange(rows, dtype=F32), GRID_W)
    col = jnp.tile(jnp.arange(GRID_W, dtype=F32), rows)
    half = HEAD_DIM // 2
    inv = ROPE_THETA ** (-jnp.arange(0, half, 2, dtype=F32) / half)
    ang_r = row[:, None] * inv
    ang_c = col[:, None] * inv
    ang = jnp.concatenate([ang_r, ang_r, ang_c, ang_c], -1)
    cos, sin = jnp.cos(ang), jnp.sin(ang)
    quarter = HEAD_DIM // 4
    first = (jnp.arange(HEAD_DIM) // quarter) % 2 == 0
    sin_up = jnp.where(first, -sin, 0.0)
    sin_dn = jnp.where(first, 0.0, sin)
    ident = jnp.zeros((tm, HEAD_DIM), F32)
    return (jnp.concatenate([cos, ident + 1.0], 0), jnp.concatenate([sin_up, ident], 0),
            jnp.concatenate([sin_dn, ident], 0))


def _inproj_even_kernel(x_ref, mod_ref, w_ref, cos_ref, sup_ref, sdn_ref, q_ref, k_ref, v_ref, z_ref, gb_ref, *,
                        conv_width):
    h = (x_ref[...] * (1.0 + mod_ref[0, 1:2, :]) + mod_ref[0, 0:1, :]).astype(BF16)
    cos, sup, sdn = cos_ref[...], sup_ref[...], sdn_ref[...]
    quarter = HEAD_DIM // 4

    def rope(u):
        return u * cos + pltpu.roll(u, HEAD_DIM - quarter, 1) * sup + pltpu.roll(u, quarter, 1) * sdn

    q = jnp.dot(h, w_ref[:, 0:ATTN_WIDTH], preferred_element_type=F32)
    for hd in range(N_Q_HEADS):
        sl = slice(hd * HEAD_DIM, (hd + 1) * HEAD_DIM)
        q_ref[:, sl] = rope(q[:, sl]).astype(BF16)
    o = ATTN_WIDTH
    kv = jnp.dot(h, w_ref[:, o:o + 2 * KV_WIDTH], preferred_element_type=F32)
    for hd in range(N_KV_HEADS):
        sl = slice(hd * HEAD_DIM, (hd + 1) * HEAD_DIM)
        k_ref[:, sl] = rope(kv[:, sl]).astype(BF16)
    v_ref[...] = kv[:, KV_WIDTH:].astype(BF16)
    o += 2 * KV_WIDTH
    u = jnp.dot(h, w_ref[:, o:o + conv_width], preferred_element_type=F32)
    gc = jnp.dot(h, w_ref[:, o + 2 * conv_width:o + 3 * conv_width], preferred_element_type=F32)
    z_ref[...] = (gc * u).astype(BF16)
    gb = jnp.dot(h, w_ref[:, o + conv_width:o + 2 * conv_width], preferred_element_type=F32)
    gb_ref[...] = gb.astype(BF16)


def _inproj_even(xs, mods, w_bf, dims):
    d, rows = dims.d, dims.r_all
    conv_width = (w_bf.shape[1] - ATTN_WIDTH - 2 * KV_WIDTH) // 3
    tm = _pick_tile(512, dims.n, dims.r_ctx)
    cos, sup, sdn = _rope_tables(dims.n, tm)
    nlat = dims.n // tm
    tab_spec = pl.BlockSpec((tm, HEAD_DIM), lambda i: (jnp.where(i * tm < dims.r_lat, i % nlat, nlat), 0))
    return pl.pallas_call(
        functools.partial(_inproj_even_kernel, conv_width=conv_width),
        grid=(rows // tm,),
        in_specs=[_row_spec(tm, d), _mod_spec(dims, tm),
                  pl.BlockSpec(w_bf.shape, lambda i: (0, 0), pipeline_mode=pl.Buffered(1)),
                  tab_spec, tab_spec, tab_spec],
        out_specs=[_row_spec(tm, ATTN_WIDTH), _row_spec(tm, KV_WIDTH), _row_spec(tm, KV_WIDTH),
                   _row_spec(tm, conv_width), _row_spec(tm, conv_width)],
        out_shape=[jax.ShapeDtypeStruct((rows, ATTN_WIDTH), BF16), jax.ShapeDtypeStruct((rows, KV_WIDTH), BF16),
                   jax.ShapeDtypeStruct((rows, KV_WIDTH), BF16), jax.ShapeDtypeStruct((rows, conv_width), BF16),
                   jax.ShapeDtypeStruct((rows, conv_width), BF16)],
        compiler_params=_cparams("parallel"),
        name="inproj_even",
    )(xs, mods, w_bf, cos, sup, sdn)


def _attn_kernel(sink_ref, q_ref, kp_ref, kc_ref, kn_ref, kx_ref, vp_ref, vc_ref, vn_ref, vx_ref, o_ref, *,
                 nb, n_ctx):
    i = pl.program_id(1)
    blk = ATTN_BLOCK
    nkeys = n_ctx + 3 * blk
    rows = Q_PER_KV * blk
    latent = i < nb
    has_prev = jnp.logical_and(latent, i > 0)
    has_next = jnp.logical_and(latent, i < nb - 1)
    col = lax.broadcasted_iota(I32, (1, nkeys), 1)
    always, never = 2 * blk, -1
    hi = jnp.where(col < n_ctx, always,
                   jnp.where(col < n_ctx + blk, jnp.where(has_prev, col - n_ctx, never),
                             jnp.where(col < n_ctx + 2 * blk, jnp.where(latent, always, never),
                                       jnp.where(has_next, always, never))))
    lo = jnp.where(col >= n_ctx + 2 * blk, col - (n_ctx + 2 * blk), 0)
    qi = lax.broadcasted_iota(I32, (rows, nkeys), 0) & (blk - 1)
    bias = jnp.where(jnp.logical_and(qi <= hi, qi >= lo), 0.0, NEG_INF).astype(F32)
    grp = lax.broadcasted_iota(I32, (rows, 1), 0) // blk
    scale = HEAD_DIM ** -0.5
    for hk in range(N_KV_HEADS):
        sl = slice(hk * HEAD_DIM, (hk + 1) * HEAD_DIM)
        kcat = jnp.concatenate([kx_ref[:, sl], kp_ref[:, sl], kc_ref[:, sl], kn_ref[:, sl]], axis=0)
        vcat = jnp.concatenate([vx_ref[:, sl], vp_ref[:, sl], vc_ref[:, sl], vn_ref[:, sl]], axis=0)
        heads = [hk * Q_PER_KV + g for g in range(Q_PER_KV)]
        qs = jnp.concatenate([q_ref[:, hq * HEAD_DIM:(hq + 1) * HEAD_DIM] for hq in heads], axis=0)
        s = lax.dot_general(qs, kcat, (((1,), (1,)), ((), ())), preferred_element_type=F32) * scale + bias
        sink = jnp.full((rows, 1), sink_ref[heads[-1]], F32)
        for g in range(Q_PER_KV - 1):
            sink = jnp.where(grp == g, sink_ref[heads[g]], sink)
        m = jnp.maximum(jnp.max(s, axis=-1, keepdims=True), sink)
        p = jnp.exp(s - m)
        den = jnp.sum(p, axis=-1, keepdims=True) + jnp.exp(sink - m)
        pn = (p * (1.0 / den)).astype(BF16)
        o = jnp.dot(pn, vcat, preferred_element_type=F32)
        for g, hq in enumerate(heads):
            o_ref[:, hq * HEAD_DIM:(hq + 1) * HEAD_DIM] = o[g * blk:(g + 1) * blk].astype(BF16)


def _attention(q, k, v, sink, dims, ctx_out):
    blk = ATTN_BLOCK
    nb = dims.n // blk
    nqc = dims.n_ctx // blk if ctx_out else 0
    rows_out = dims.r_all if ctx_out else dims.r_lat
    lat_blocks = dims.r_lat // blk

    def q_map(b, i, *_):
        return (jnp.where(i < nb, b * nb + i, lat_blocks + b * nqc + (i - nb)), 0)

    def k_map(shift):
        return lambda b, i, *_: (b * nb + jnp.clip(i + shift, 0, nb - 1), 0)

    def ctx_map(b, i, *_):
        return (dims.r_lat // dims.n_ctx + b, 0)

    kv_specs = [pl.BlockSpec((blk, KV_WIDTH), k_map(-1)), pl.BlockSpec((blk, KV_WIDTH), k_map(0)),
                pl.BlockSpec((blk, KV_WIDTH), k_map(1)), pl.BlockSpec((dims.n_ctx, KV_WIDTH), ctx_map)]
    return pl.pallas_call(
        functools.partial(_attn_kernel, nb=nb, n_ctx=dims.n_ctx),
        grid=(dims.bsz, nb + nqc),
        in_specs=[pl.BlockSpec(memory_space=pltpu.SMEM), pl.BlockSpec((blk, ATTN_WIDTH), q_map)]
        + kv_specs + kv_specs,
        out_specs=pl.BlockSpec((blk, ATTN_WIDTH), q_map),
        out_shape=jax.ShapeDtypeStruct((rows_out, ATTN_WIDTH), BF16),
        compiler_params=_cparams("parallel", "parallel"),
        name="band_attention",
    )(sink, q, k, k, k, k, v, v, v, v)


def _halo_specs(tm, width, total_rows):
    per = tm // HALO_ROWS
    last = total_rows // HALO_ROWS - 1
    prev = pl.BlockSpec((HALO_ROWS, width), lambda i, *_: (jnp.maximum(i * per - 1, 0), 0))
    nxt = pl.BlockSpec((HALO_ROWS, width), lambda i, *_: (jnp.minimum((i + 1) * per, last), 0))
    return prev, nxt


def _outproj_even_kernel(a_ref, z_ref, zp_ref, zn_ref, gb_ref, cw_ref, w_ref, x_ref, mod_ref, g_ref, b_ref,
                         o_ref, cat_ref, *, dims, alpha):
    tm = x_ref.shape[0]
    row = lax.broadcasted_iota(I32, (tm, 1), 0)
    pos, length = _seq_position(pl.program_id(0) * tm + row, dims)
    z = z_ref[...].astype(F32)
    zprev = jnp.where(row == 0, zp_ref[HALO_ROWS - 1:HALO_ROWS, :].astype(F32), pltpu.roll(z, 1, 0))
    zprev = jnp.where(pos == 0, 0.0, zprev)
    znext = jnp.where(row == tm - 1, zn_ref[0:1, :].astype(F32), pltpu.roll(z, tm - 1, 0))
    znext = jnp.where(pos == length - 1, 0.0, znext)
    y = zprev * cw_ref[0:1, :] + z * cw_ref[1:2, :] + znext * cw_ref[2:3, :]
    cat_ref[:, 0:ATTN_WIDTH] = a_ref[...]
    cat_ref[:, ATTN_WIDTH:] = (gb_ref[...].astype(F32) * y).astype(BF16)
    mix = jnp.dot(cat_ref[...], w_ref[...], preferred_element_type=F32)
    o_ref[...] = _residual_ln(x_ref[...], mix, mod_ref[0, 2:3, :], g_ref[...], b_ref[...], alpha)


def _outproj_even(a, z, gb, conv_w, w_bf, xs, mods, ln_g, ln_b, dims, rows, alpha):
    d = dims.d
    conv_width = z.shape[1]
    tm = _pick_tile(512, dims.n, dims.r_ctx)
    zp_spec, zn_spec = _halo_specs(tm, conv_width, z.shape[0])
    return pl.pallas_call(
        functools.partial(_outproj_even_kernel, dims=dims, alpha=alpha),
        grid=(rows // tm,),
        in_specs=[_row_spec(tm, ATTN_WIDTH), _row_spec(tm, conv_width), zp_spec, zn_spec,
                  _row_spec(tm, conv_width), _const_spec(conv_w.shape), _const_spec(w_bf.shape),
                  _row_spec(tm, d), _mod_spec(dims, tm), _const_spec((1, d)), _const_spec((1, d))],
        out_specs=_row_spec(tm, d),
        out_shape=jax.ShapeDtypeStruct((rows, d), F32),
        scratch_shapes=[pltpu.VMEM((tm, d), BF16)],
        compiler_params=_cparams("parallel"),
        name="outproj_even",
    )(a, z, z, z, gb, conv_w, w_bf, xs, mods, ln_g, ln_b)


def _ffn_kernel(x_ref, mod_ref, wg_ref, wu_ref, wd_ref, g_ref, b_ref, o_ref, h_ref, acc_ref, *, alpha):
    k = pl.program_id(1)

    @pl.when(k == 0)
    def _():
        h_ref[...] = (x_ref[...] * (1.0 + mod_ref[0, 4:5, :]) + mod_ref[0, 3:4, :]).astype(BF16)
        acc_ref[...] = jnp.zeros_like(acc_ref)

    h = h_ref[...]
    gate = jnp.dot(h, wg_ref[...], preferred_element_type=F32)
    up = jnp.dot(h, wu_ref[...], preferred_element_type=F32)
    act = (_silu(gate) * up).astype(BF16)
    acc_ref[...] += jnp.dot(act, wd_ref[...], preferred_element_type=F32)

    @pl.when(k == pl.num_programs(1) - 1)
    def _():
        o_ref[...] = _residual_ln(x_ref[...], acc_ref[...], mod_ref[0, 5:6, :], g_ref[...], b_ref[...], alpha)


def _ffn(xs, mods, wg, wu, wd, ln_g, ln_b, dims, rows, alpha):
    d, dff = wg.shape
    tm = _pick_tile(512, dims.n, dims.r_ctx)
    tf = _pick_tile(512, dff)
    return pl.pallas_call(
        functools.partial(_ffn_kernel, alpha=alpha),
        grid=(rows // tm, dff // tf),
        in_specs=[pl.BlockSpec((tm, d), lambda i, k: (i, 0)), _mod_spec(dims, tm),
                  pl.BlockSpec((d, tf), lambda i, k: (0, k)), pl.BlockSpec((d, tf), lambda i, k: (0, k)),
                  pl.BlockSpec((tf, d), lambda i, k: (k, 0)),
                  _const_spec((1, d)), _const_spec((1, d))],
        out_specs=pl.BlockSpec((tm, d), lambda i, k: (i, 0)),
        out_shape=jax.ShapeDtypeStruct((rows, d), F32),
        scratch_shapes=[pltpu.VMEM((tm, d), BF16), pltpu.VMEM((tm, d), F32)],
        compiler_params=_cparams("parallel", "arbitrary"),
        name="dense_ffn",
    )(xs, mods, wg, wu, wd, ln_g, ln_b)


def _inproj_odd_kernel(x_ref, mod_ref, w_ref, up_ref, uf_ref):
    h = (x_ref[...] * (1.0 + mod_ref[0, 1:2, :]) + mod_ref[0, 0:1, :]).astype(BF16)
    pw = up_ref.shape[1]
    up_ref[...] = jnp.dot(h, w_ref[:, 0:pw], preferred_element_type=F32).astype(BF16)
    uf_ref[...] = jnp.dot(h, w_ref[:, pw:], preferred_element_type=F32).astype(BF16)


def _inproj_odd(xs, mods, w_bf, dims, rows, pool_width):
    d = dims.d
    four_width = w_bf.shape[1] - pool_width
    tm = _pick_tile(512, dims.n, dims.r_ctx)
    return pl.pallas_call(
        _inproj_odd_kernel,
        grid=(rows // tm,),
        in_specs=[_row_spec(tm, d), _mod_spec(dims, tm), _const_spec(w_bf.shape)],
        out_specs=[_row_spec(tm, pool_width), _row_spec(tm, four_width)],
        out_shape=[jax.ShapeDtypeStruct((rows, pool_width), BF16), jax.ShapeDtypeStruct((rows, four_width), BF16)],
        compiler_params=_cparams("parallel"),
        name="inproj_odd",
    )(xs, mods, w_bf)


def _seq_dft_constants(n, group):
    n2 = n // FFT_RADIX
    b = np.arange(n2, dtype=np.float64)
    ka = np.arange(FFT_RADIX, dtype=np.float64)
    ang = 2.0 * np.pi * np.outer(b, ka) / n
    tw = np.concatenate([np.cos(ang), -np.sin(ang)], axis=1)
    ang2 = 2.0 * np.pi * np.outer(b, b) / n2
    cs, sn = np.cos(ang2), np.sin(ang2)
    mat = np.block([[cs, sn], [-sn, cs]]) / np.sqrt(float(n) * group)
    return jnp.asarray(tw, F32), jnp.asarray(mat, BF16)


def _seq_dft_kernel(x_ref, tw_ref, m_ref, or_ref, oi_ref):
    n2 = x_ref.shape[1]
    lanes = x_ref.shape[2]
    for ka in range(FFT_RADIX):
        yr = yi = None
        for a in range(FFT_RADIX):
            ang = 2.0 * np.pi * ((a * ka) % FFT_RADIX) / FFT_RADIX
            cr, ci = float(np.round(np.cos(ang), 12)), float(np.round(-np.sin(ang), 12))
            xa = x_ref[a].astype(F32)
            if cr != 0.0:
                yr = cr * xa if yr is None else yr + cr * xa
            if ci != 0.0:
                yi = ci * xa if yi is None else yi + ci * xa
        if ka == 0:
            zr, zi = yr, jnp.zeros_like(yr)
        else:
            twr = jnp.broadcast_to(tw_ref[:, ka:ka + 1], (n2, lanes))
            twi = jnp.broadcast_to(tw_ref[:, FFT_RADIX + ka:FFT_RADIX + ka + 1], (n2, lanes))
            if yi is None:
                zr, zi = yr * twr, yr * twi
            else:
                zr, zi = yr * twr - yi * twi, yr * twi + yi * twr
        zcat = jnp.concatenate([zr, zi], axis=0).astype(BF16)
        res = jnp.dot(m_ref[...], zcat, preferred_element_type=F32)
        or_ref[ka] = res[0:n2].astype(BF16)
        oi_ref[ka] = res[n2:].astype(BF16)


def _seq_dft(uf, row0, bsz, n, group):
    rows, width = uf.shape
    n2 = n // FFT_RADIX
    lanes = _pick_tile(FFT_LANES, width)
    tw, mat = _seq_dft_constants(n, group)
    x3 = uf.reshape(rows // n2, n2, width)
    slab0 = row0 // n
    out = jax.ShapeDtypeStruct((bsz, FFT_RADIX, n2, width), BF16)
    ospec = pl.BlockSpec((None, FFT_RADIX, n2, lanes), lambda b, c: (b, 0, 0, c))
    return pl.pallas_call(
        _seq_dft_kernel,
        grid=(bsz, width // lanes),
        in_specs=[pl.BlockSpec((FFT_RADIX, n2, lanes), lambda b, c: (slab0 + b, 0, c)),
                  pl.BlockSpec(tw.shape, lambda b, c: (0, 0)),
                  pl.BlockSpec(mat.shape, lambda b, c: (0, 0), pipeline_mode=pl.Buffered(1))],
        out_specs=[ospec, ospec],
        out_shape=[out, out],
        compiler_params=_cparams("parallel", "parallel"),
        name="seq_dft",
    )(x3, tw, mat)


def _top2_routing(logits, n_exp):
    lane = lax.broadcasted_iota(I32, logits.shape, 1)
    lg = jnp.where(lane < n_exp, logits, -jnp.inf)
    m1 = jnp.max(lg, axis=-1, keepdims=True)
    i1 = jnp.min(jnp.where(lg == m1, lane, ROUTE_LANES), axis=-1, keepdims=True)
    lg2 = jnp.where(lane == i1, -jnp.inf, lg)
    m2 = jnp.max(lg2, axis=-1, keepdims=True)
    i2 = jnp.min(jnp.where(lg2 == m2, lane, ROUTE_LANES), axis=-1, keepdims=True)
    e2 = jnp.exp(m2 - m1)
    g1 = 1.0 / (1.0 + e2)
    g2 = e2 * g1
    return jnp.where(lane == 0, i1.astype(F32),
                     jnp.where(lane == 1, i2.astype(F32), jnp.where(lane == 2, g1, jnp.where(lane == 3, g2, 0.0))))


def _outproj_odd_kernel(*refs, dims, alpha, has_ctx, n_exp):
    (up_ref, upp_ref, upn_ref, xr_ref, xi_ref) = refs[:5]
    refs = refs[5:]
    if has_ctx:
        (cr_ref, ci_ref) = refs[:2]
        refs = refs[2:]
    (perm_ref, cs_ref, fw_ref, pw_ref, ps_ref, w_ref, x_ref, mod_ref, g_ref, b_ref, rw_ref,
     o_ref, h2_ref, route_ref, ext_ref, cat_ref) = refs
    tm = x_ref.shape[0]
    i = pl.program_id(0)
    row = lax.broadcasted_iota(I32, (tm, 1), 0)
    pos, length = _seq_position(i * tm + row, dims)
    first_pos, _ = _seq_position(i * tm, dims)
    last_pos, last_len = _seq_position(i * tm + tm - 1, dims)

    pool_width = up_ref.shape[1]
    group = pool_width // len(POOL_WINDOWS)
    h0 = POOL_HALO
    ext_ref[0:h0, :] = jnp.where(first_pos == 0, 0.0, upp_ref[HALO_ROWS - h0:HALO_ROWS, :].astype(F32))
    ext_ref[h0:h0 + tm, :] = up_ref[...].astype(F32)
    ext_ref[h0 + tm:h0 + tm + h0, :] = jnp.where(last_pos == last_len - 1, 0.0, upn_ref[0:h0, :].astype(F32))
    for gi, w in enumerate(POOL_WINDOWS):
        cols = slice(gi * group, (gi + 1) * group)
        back, ahead = w // 2, w - w // 2 - 1
        tot = None
        for s in range(-back, ahead + 1):
            t = ext_ref[h0 + s:h0 + s + tm, cols]
            tot = t if tot is None else tot + t
        cnt = jnp.minimum(pos + ahead, length - 1) - jnp.maximum(pos - back, 0) + 1
        pooled = tot / cnt.astype(F32) - ext_ref[h0:h0 + tm, cols]
        yp = jnp.dot(pooled.astype(BF16), pw_ref[gi], preferred_element_type=F32) * ps_ref[:, cols]
        cat_ref[:, cols] = yp.astype(BF16)

    if has_ctx:
        lat = i * tm < dims.r_lat
        xr = jnp.where(lat, xr_ref[...], cr_ref[...])
        xi = jnp.where(lat, xi_ref[...], ci_ref[...])
    else:
        xr, xi = xr_ref[...], xi_ref[...]
    four_width = xr.shape[-1]
    fgroup = four_width // FOURIER_HEADS
    xr = jnp.dot(perm_ref[...], xr.reshape(tm, four_width), preferred_element_type=F32).astype(BF16)
    xi = jnp.dot(perm_ref[...], xi.reshape(tm, four_width), preferred_element_type=F32).astype(BF16)
    for hd in range(FOURIER_HEADS):
        cols = slice(hd * fgroup, (hd + 1) * fgroup)
        both = jnp.concatenate([xr[:, cols], xi[:, cols]], axis=1)
        f = jnp.dot(both, cs_ref[...], preferred_element_type=F32)
        yf = jnp.dot(f.astype(BF16), fw_ref[hd], preferred_element_type=F32)
        cat_ref[:, pool_width + hd * fgroup:pool_width + (hd + 1) * fgroup] = yf.astype(BF16)

    mix = jnp.dot(cat_ref[...], w_ref[...], preferred_element_type=F32)
    xn = _residual_ln(x_ref[...], mix, mod_ref[0, 2:3, :], g_ref[...], b_ref[...], alpha)
    o_ref[...] = xn
    h2 = xn * (1.0 + mod_ref[0, 4:5, :]) + mod_ref[0, 3:4, :]
    h2_ref[...] = h2
    logits = jnp.dot(h2, rw_ref[...], preferred_element_type=F32, precision=lax.Precision.HIGHEST)
    route_ref[...] = _top2_routing(logits, n_exp)


def _channel_dft_matrix(group):
    c = np.arange(group, dtype=np.float64)
    ang = 2.0 * np.pi * np.outer(c, c) / group
    return jnp.asarray(np.concatenate([np.cos(ang), np.sin(ang)], axis=0), BF16)


def _tile_permutation(tm):
    per = tm // FFT_RADIX
    p = np.zeros((tm, tm), np.float32)
    for ka in range(FFT_RADIX):
        for kb in range(per):
            p[kb * FFT_RADIX + ka, ka * per + kb] = 1.0
    return jnp.asarray(p, BF16)


def _outproj_odd(up, lat_ri, ctx_ri, fourier_w, pool_w, pool_scale, w_bf, xs, mods, ln_g, ln_b, router_w,
                 dims, rows, alpha):
    d = dims.d
    tm = SEQ_TILE
    pool_width = up.shape[1]
    four_width = lat_ri[0].shape[-1]
    fgroup = four_width // FOURIER_HEADS
    has_ctx = ctx_ri is not None
    n_exp = router_w.shape[1]
    per = tm // FFT_RADIX
    lat_tiles = dims.n // tm
    n_lat = dims.r_lat // tm
    upp_spec, upn_spec = _halo_specs(tm, pool_width, up.shape[0])

    def lat_map(i):
        j = jnp.minimum(i, n_lat - 1)
        return (j // lat_tiles, 0, j % lat_tiles, 0)

    lat_spec = pl.BlockSpec((None, FFT_RADIX, per, four_width), lat_map)
    operands = [up, up, up, lat_ri[0], lat_ri[1]]
    specs = [_row_spec(tm, pool_width), upp_spec, upn_spec, lat_spec, lat_spec]
    if has_ctx:
        ctx_tiles = dims.n_ctx // tm

        def ctx_map(i):
            j = jnp.maximum(i - n_lat, 0)
            return (j // ctx_tiles, 0, j % ctx_tiles, 0)

        ctx_spec = pl.BlockSpec((None, FFT_RADIX, per, four_width), ctx_map)
        operands += [ctx_ri[0], ctx_ri[1]]
        specs += [ctx_spec, ctx_spec]
    rw = jnp.zeros((d, ROUTE_LANES), F32).at[:, :n_exp].set(router_w)
    consts = [_tile_permutation(tm), _channel_dft_matrix(fgroup), fourier_w, pool_w, pool_scale, w_bf]
    operands += consts + [xs, mods, ln_g, ln_b, rw]
    specs += [_const_spec(a.shape) for a in consts]
    specs += [_row_spec(tm, d), _mod_spec(dims, tm), _const_spec((1, d)), _const_spec((1, d)),
              _const_spec(rw.shape)]
    return pl.pallas_call(
        functools.partial(_outproj_odd_kernel, dims=dims, alpha=alpha, has_ctx=has_ctx, n_exp=n_exp),
        grid=(rows // tm,),
        in_specs=specs,
        out_specs=[_row_spec(tm, d), _row_spec(tm, d), _row_spec(tm, ROUTE_LANES)],
        out_shape=[jax.ShapeDtypeStruct((rows, d), F32), jax.ShapeDtypeStruct((rows, d), F32),
                   jax.ShapeDtypeStruct((rows, ROUTE_LANES), F32)],
        scratch_shapes=[pltpu.VMEM((tm + 2 * POOL_HALO, pool_width), F32), pltpu.VMEM((tm, d), BF16)],
        compiler_params=_cparams("parallel"),
        name="outproj_odd",
    )(*operands)


def _dispatch_plan(route, n_exp, tile):
    tokens = route.shape[0]
    assign = TOP_K_ASSIGN * tokens
    experts = route[:, 0:TOP_K_ASSIGN].astype(I32).reshape(assign)
    onehot = (experts[:, None] == jnp.arange(n_exp, dtype=I32)[None, :]).astype(I32)
    running = jnp.cumsum(onehot, axis=0)
    rank = jnp.sum((running - onehot) * onehot, axis=1)
    counts = running[-1]
    padded = ((counts + tile - 1) // tile) * tile
    ends = jnp.cumsum(padded)
    starts = ends - padded
    slot = jnp.sum(onehot * starts[None, :], axis=1) + rank
    n_slots = -(-(assign + n_exp * (tile - 1)) // tile) * tile
    source = jnp.zeros((n_slots,), I32).at[slot].set(jnp.arange(assign, dtype=I32) // TOP_K_ASSIGN)
    tile_start = jnp.arange(n_slots // tile, dtype=I32) * tile
    tile_expert = jnp.minimum(jnp.sum((tile_start[:, None] >= ends[None, :]).astype(I32), axis=1), n_exp - 1)
    valid = tile_start < ends[-1]
    last_valid = tile_expert[jnp.maximum(ends[-1] // tile - 1, 0)]
    tile_expert = jnp.where(valid, tile_expert, last_valid)
    return slot, source, tile_expert, valid.astype(I32)


TOP_K_ASSIGN = 2


def _gather_kernel(src_ref, h_ref, o_ref, sem):
    base = pl.program_id(0) * GATHER_ROWS

    def issue(r, carry):
        pltpu.make_async_copy(h_ref.at[pl.ds(src_ref[base + r], 1)], o_ref.at[pl.ds(r, 1)], sem).start()
        return carry

    lax.fori_loop(0, GATHER_ROWS, issue, 0)
    pltpu.make_async_copy(h_ref.at[pl.ds(0, GATHER_ROWS)], o_ref, sem).wait()


def _gather_rows(source, h2):
    n_slots = source.shape[0]
    d = h2.shape[1]
    return pl.pallas_call(
        _gather_kernel,
        grid_spec=pltpu.PrefetchScalarGridSpec(
            num_scalar_prefetch=1,
            grid=(n_slots // GATHER_ROWS,),
            in_specs=[pl.BlockSpec(memory_space=pl.ANY)],
            out_specs=pl.BlockSpec((GATHER_ROWS, d), lambda i, src: (i, 0)),
            scratch_shapes=[pltpu.SemaphoreType.DMA(())]),
        out_shape=jax.ShapeDtypeStruct((n_slots, d), h2.dtype),
        compiler_params=_cparams("arbitrary"),
        name="moe_gather",
    )(source, h2)


def _expert_kernel(te_ref, valid_ref, x_ref, wg_ref, wu_ref, wd_ref, o_ref, h_ref, acc_ref):
    i, k = pl.program_id(0), pl.program_id(1)
    last = pl.num_programs(1) - 1
    live = valid_ref[i] == 1

    @pl.when(jnp.logical_and(live, k == 0))
    def _():
        h_ref[...] = x_ref[...].astype(BF16)
        acc_ref[...] = jnp.zeros_like(acc_ref)

    @pl.when(live)
    def _():
        h = h_ref[...]
        gate = jnp.dot(h, wg_ref[...], preferred_element_type=F32)
        up = jnp.dot(h, wu_ref[...], preferred_element_type=F32)
        act = (_silu(gate) * up).astype(BF16)
        acc_ref[...] += jnp.dot(act, wd_ref[...], preferred_element_type=F32)

    @pl.when(jnp.logical_and(live, k == last))
    def _():
        o_ref[...] = acc_ref[...]

    @pl.when(jnp.logical_and(jnp.logical_not(live), k == last))
    def _():
        o_ref[...] = jnp.zeros_like(o_ref)


def _expert_ffn(tile_expert, valid, xs, wg, wu, wd):
    n_slots, d = xs.shape
    dff = wg.shape[2]
    tm = EXPERT_TILE
    tf = _pick_tile(512, dff)
    nk = dff // tf

    def chunk(i, k, te, va):
        return jnp.where(va[i] == 1, k, nk - 1)

    return pl.pallas_call(
        _expert_kernel,
        grid_spec=pltpu.PrefetchScalarGridSpec(
            num_scalar_prefetch=2,
            grid=(n_slots // tm, nk),
            in_specs=[pl.BlockSpec((tm, d), lambda i, k, te, va: (i, 0)),
                      pl.BlockSpec((None, d, tf), lambda i, k, te, va: (te[i], 0, chunk(i, k, te, va))),
                      pl.BlockSpec((None, d, tf), lambda i, k, te, va: (te[i], 0, chunk(i, k, te, va))),
                      pl.BlockSpec((None, tf, d), lambda i, k, te, va: (te[i], chunk(i, k, te, va), 0))],
            out_specs=pl.BlockSpec((tm, d), lambda i, k, te, va: (i, 0)),
            scratch_shapes=[pltpu.VMEM((tm, d), BF16), pltpu.VMEM((tm, d), F32)]),
        out_shape=jax.ShapeDtypeStruct((n_slots, d), F32),
        compiler_params=_cparams("arbitrary", "arbitrary"),
        name="expert_ffn",
    )(tile_expert, valid, xs, wg, wu, wd)


def _combine_kernel(slot_ref, y_ref, route_ref, x_ref, mod_ref, g_ref, b_ref, o_ref, buf_ref, sem, *, alpha):
    tm = x_ref.shape[0]
    base = pl.program_id(0) * tm * TOP_K_ASSIGN

    def issue(r, carry):
        for c in range(TOP_K_ASSIGN):
            src = slot_ref[base + r * TOP_K_ASSIGN + c]
            pltpu.make_async_copy(y_ref.at[pl.ds(src, 1)], buf_ref.at[c, pl.ds(r, 1)], sem).start()
        return carry

    lax.fori_loop(0, tm, issue, 0)
    for c in range(TOP_K_ASSIGN):
        pltpu.make_async_copy(y_ref.at[pl.ds(0, tm)], buf_ref.at[c], sem).wait()
    route = route_ref[...]
    ff = route[:, 2:3] * buf_ref[0] + route[:, 3:4] * buf_ref[1]
    o_ref[...] = _residual_ln(x_ref[...], ff, mod_ref[0, 5:6, :], g_ref[...], b_ref[...], alpha)


def _combine(slot, ys, route, xs, mods, ln_g, ln_b, dims, rows, alpha):
    d = dims.d
    tm = SEQ_TILE
    return pl.pallas_call(
        functools.partial(_combine_kernel, alpha=alpha),
        grid_spec=pltpu.PrefetchScalarGridSpec(
            num_scalar_prefetch=1,
            grid=(rows // tm,),
            in_specs=[pl.BlockSpec(memory_space=pl.ANY), _row_spec(tm, ROUTE_LANES), _row_spec(tm, d),
                      _mod_spec(dims, tm), _const_spec((1, d)), _const_spec((1, d))],
            out_specs=_row_spec(tm, d),
            scratch_shapes=[pltpu.VMEM((TOP_K_ASSIGN, tm, d), F32), pltpu.SemaphoreType.DMA(())]),
        out_shape=jax.ShapeDtypeStruct((rows, d), F32),
        compiler_params=_cparams("arbitrary"),
        name="moe_combine",
    )(slot, ys, route, xs, mods, ln_g, ln_b)


def _moe(xs, h2, route, mods, wg, wu, wd, ln_g, ln_b, dims, rows, alpha):
    n_exp = wg.shape[0]
    slot, source, tile_expert, valid = _dispatch_plan(route, n_exp, EXPERT_TILE)
    gathered = _gather_rows(source, h2)
    ys = _expert_ffn(tile_expert, valid, gathered, wg, wu, wd)
    return _combine(slot, ys, route, xs, mods, ln_g, ln_b, dims, rows, alpha)


def kernel(x, c, ctx, c_ctx, w_mod, b_mod, w_mix_out, ln_g, ln_b, w_in_ab, conv_w, attn_sink, w_in_cd, pool_w,
           pool_scale, fourier_w, ffn_w_gate, ffn_w_up, ffn_w_down, router_w, moe_w_gate, moe_w_up, moe_w_down):
    bsz, n, d = x.shape
    n_ctx = ctx.shape[1]
    depth = w_mod.shape[0]
    dims = Dims(bsz, n, n_ctx, d)
    assert n % SEQ_TILE == 0 and n_ctx % SEQ_TILE == 0 and n % GRID_W == 0
    assert dims.r_lat % (FFT_RADIX * (n_ctx // FFT_RADIX)) == 0 and dims.r_all % (n // FFT_RADIX) == 0
    alpha = (2.0 * depth) ** 0.25
    pool_width = pool_w.shape[1] * pool_w.shape[2]
    fgroup = fourier_w.shape[2]

    mods_all = _modulations(c, c_ctx, w_mod, b_mod)
    xs = jnp.concatenate([x.reshape(dims.r_lat, d), ctx.reshape(dims.r_ctx, d)], axis=0)
    for l in range(depth):
        even = l % 2 == 0
        j = l // 2
        ctx_out = any(m % 2 == 0 for m in range(l + 1, depth))
        rows = dims.r_all if ctx_out else dims.r_lat
        mods = mods_all[l]
        w_out = w_mix_out[l].astype(BF16)
        g1, b1 = ln_g[l, 0][None, :], ln_b[l, 0][None, :]
        g2, b2 = ln_g[l, 1][None, :], ln_b[l, 1][None, :]
        if even:
            q, k, v, z, gb = _inproj_even(xs, mods, w_in_ab[j].astype(BF16), dims)
            a = _attention(q, k, v, attn_sink[j], dims, ctx_out)
            xs = _outproj_even(a, z, gb, conv_w[j], w_out, xs, mods, g1, b1, dims, rows, alpha)
            xs = _ffn(xs, mods, ffn_w_gate[j].astype(BF16), ffn_w_up[j].astype(BF16),
                      ffn_w_down[j].astype(BF16), g2, b2, dims, rows, alpha)
        else:
            up, uf = _inproj_odd(xs, mods, w_in_cd[j].astype(BF16), dims, rows, pool_width)
            lat_ri = _seq_dft(uf, 0, bsz, n, fgroup)
            ctx_ri = _seq_dft(uf, dims.r_lat, bsz, n_ctx, fgroup) if ctx_out else None
            xs, h2, route = _outproj_odd(up, lat_ri, ctx_ri, fourier_w[j].astype(BF16), pool_w[j].astype(BF16),
                                         pool_scale[j][None, :], w_out, xs, mods, g1, b1, router_w[j],
                                         dims, rows, alpha)
            xs = _moe(xs, h2, route, mods, moe_w_gate[j].astype(BF16), moe_w_up[j].astype(BF16),
                      moe_w_down[j].astype(BF16), g2, b2, dims, rows, alpha)
    return xs[:dims.r_lat].reshape(bsz, n, d)
```

```python
import functools
from typing import NamedTuple

import numpy as np
import jax
import jax.numpy as jnp
from jax import lax
from jax.experimental import pallas as pl
from jax.experimental.pallas import tpu as pltpu

F32 = jnp.float32
BF16 = jnp.bfloat16
I32 = jnp.int32

HEAD_DIM = 128
N_Q_HEADS = 8
N_KV_HEADS = 2
Q_PER_KV = N_Q_HEADS // N_KV_HEADS
ATTN_WIDTH = N_Q_HEADS * HEAD_DIM
KV_WIDTH = N_KV_HEADS * HEAD_DIM
ATTN_BLOCK = 128
GRID_W = 64
ROPE_THETA = 10000.0
NEG_INF = -1e30
POOL_WINDOWS = (2, 4, 8, 16)
POOL_HALO = 8
HALO_ROWS = 16
FOURIER_HEADS = 4
N_MOD = 6
LN_EPS = 1e-5
TOP_K = 2
FFT_RADIX = 8
FFT_LANES = 256
SEQ_TILE = 256
ROUTE_LANES = 128
EXPERT_TILE = 512
ZERO_ROWS = 256
SUBLANES = 8
PAD_LIVE = 4
V7X_VMEM_LIMIT = 56 * 1024 * 1024


class Dims(NamedTuple):
    bsz: int
    n: int
    n_ctx: int
    d: int

    @property
    def r_lat(self):
        return self.bsz * self.n

    @property
    def r_ctx(self):
        return self.bsz * self.n_ctx

    @property
    def r_all(self):
        return self.r_lat + self.r_ctx


def _pick_tile(pref, *dims):
    t = pref
    while any(d % t for d in dims):
        t //= 2
    return t


def _cparams(*sem, unchecked=False):
    return pltpu.CompilerParams(dimension_semantics=sem, vmem_limit_bytes=V7X_VMEM_LIMIT,
                                disable_bounds_checks=unchecked)


def _residual_ln(x, y, gate, g, b, alpha):
    v = alpha * x + gate * y
    mu = jnp.mean(v, axis=-1, keepdims=True)
    vc = v - mu
    var = jnp.mean(vc * vc, axis=-1, keepdims=True)
    return vc * lax.rsqrt(var + LN_EPS) * g + b


def _silu(t):
    return t * jax.nn.sigmoid(t)


def _seq_position(row, dims):
    lat = row < dims.r_lat
    pos = jnp.where(lat, lax.rem(row, dims.n), lax.rem(row - dims.r_lat, dims.n_ctx))
    length = jnp.where(lat, dims.n, dims.n_ctx)
    return pos, length


def _mod_spec(dims, tm, layer):
    return pl.BlockSpec((None, 1, N_MOD, dims.d), lambda i, *_: (layer, (i * tm) // dims.n, 0, 0))


def _row_spec(tm, width):
    return pl.BlockSpec((tm, width), lambda i, *_: (i, 0))


def _layer_spec(arr, index, **kw):
    zeros = (0,) * (arr.ndim - 1)
    return pl.BlockSpec((None,) + arr.shape[1:], lambda *_: (index,) + zeros, **kw)


def _const_spec(shape):
    zeros = (0,) * len(shape)
    return pl.BlockSpec(shape, lambda *_: zeros)


def _mod_kernel(cond_ref, w_ref, b_ref, o_ref):
    s = _silu(cond_ref[...]).astype(BF16)
    o_ref[...] = jnp.dot(s, w_ref[...].astype(BF16), preferred_element_type=F32) + b_ref[...]


def _modulations(c, c_ctx, w_mod, b_mod):
    depth, d, nmod = w_mod.shape
    bsz = c.shape[0]
    rows = -(-(bsz + 1) // 8) * 8
    cond = jnp.zeros((rows, d), F32).at[:bsz].set(c).at[bsz].set(c_ctx)
    tn = _pick_tile(1024, nmod)
    out = pl.pallas_call(
        _mod_kernel,
        grid=(depth, nmod // tn),
        in_specs=[pl.BlockSpec((rows, d), lambda l, j: (0, 0)),
                  pl.BlockSpec((None, d, tn), lambda l, j: (l, 0, j)),
                  pl.BlockSpec((None, 1, tn), lambda l, j: (l, 0, j))],
        out_specs=pl.BlockSpec((None, rows, tn), lambda l, j: (l, 0, j)),
        out_shape=jax.ShapeDtypeStruct((depth, rows, nmod), F32),
        compiler_params=_cparams("arbitrary", "arbitrary"),
        name="modulation",
    )(cond, w_mod, b_mod.reshape(depth, 1, nmod))
    return out.reshape(depth, rows, N_MOD, d)


def _rope_tables(n, tm):
    rows = n // GRID_W
    row = jnp.repeat(jnp.arange(rows, dtype=F32), GRID_W)
    col = jnp.tile(jnp.arange(GRID_W, dtype=F32), rows)
    half = HEAD_DIM // 2
    inv = ROPE_THETA ** (-jnp.arange(0, half, 2, dtype=F32) / half)
    ang_r = row[:, None] * inv
    ang_c = col[:, None] * inv
    ang = jnp.concatenate([ang_r, ang_r, ang_c, ang_c], -1)
    cos, sin = jnp.cos(ang), jnp.sin(ang)
    quarter = HEAD_DIM // 4
    first = (jnp.arange(HEAD_DIM) // quarter) % 2 == 0
    sin_up = jnp.where(first, -sin, 0.0)
    sin_dn = jnp.where(first, 0.0, sin)
    ident = jnp.zeros((tm, HEAD_DIM), F32)
    return (jnp.concatenate([cos, ident + 1.0], 0), jnp.concatenate([sin_up, ident], 0),
            jnp.concatenate([sin_dn, ident], 0))


def _inproj_even_kernel(x_ref, mod_ref, w_ref, cos_ref, sup_ref, sdn_ref, q_ref, k_ref, v_ref, z_ref, gb_ref, *,
                        conv_width):
    h = (x_ref[...] * (1.0 + mod_ref[0, 1:2, :]) + mod_ref[0, 0:1, :]).astype(BF16)
    cos, sup, sdn = cos_ref[...], sup_ref[...], sdn_ref[...]
    quarter = HEAD_DIM // 4

    def rope(u):
        return u * cos + pltpu.roll(u, HEAD_DIM - quarter, 1) * sup + pltpu.roll(u, quarter, 1) * sdn

    q = jnp.dot(h, w_ref[:, 0:ATTN_WIDTH], preferred_element_type=F32)
    for hd in range(N_Q_HEADS):
        sl = slice(hd * HEAD_DIM, (hd + 1) * HEAD_DIM)
        q_ref[:, sl] = rope(q[:, sl]).astype(BF16)
    o = ATTN_WIDTH
    kv = jnp.dot(h, w_ref[:, o:o + 2 * KV_WIDTH], preferred_element_type=F32)
    for hd in range(N_KV_HEADS):
        sl = slice(hd * HEAD_DIM, (hd + 1) * HEAD_DIM)
        k_ref[:, sl] = rope(kv[:, sl]).astype(BF16)
    v_ref[...] = kv[:, KV_WIDTH:].astype(BF16)
    o += 2 * KV_WIDTH
    u = jnp.dot(h, w_ref[:, o:o + conv_width], preferred_element_type=F32)
    gc = jnp.dot(h, w_ref[:, o + 2 * conv_width:o + 3 * conv_width], preferred_element_type=F32)
    z_ref[...] = (gc * u).astype(BF16)
    gb = jnp.dot(h, w_ref[:, o + conv_width:o + 2 * conv_width], preferred_element_type=F32)
    gb_ref[...] = gb.astype(BF16)


def _inproj_even(xs, mods, w_in, dims, layer, j):
    d, rows = dims.d, dims.r_all
    conv_width = (w_in.shape[2] - ATTN_WIDTH - 2 * KV_WIDTH) // 3
    tm = _pick_tile(512, dims.n, dims.r_ctx)
    cos, sup, sdn = _rope_tables(dims.n, tm)
    nlat = dims.n // tm
    tab_spec = pl.BlockSpec((tm, HEAD_DIM), lambda i: (jnp.where(i * tm < dims.r_lat, i % nlat, nlat), 0))
    return pl.pallas_call(
        functools.partial(_inproj_even_kernel, conv_width=conv_width),
        grid=(rows // tm,),
        in_specs=[_row_spec(tm, d), _mod_spec(dims, tm, layer),
                  _layer_spec(w_in, j, pipeline_mode=pl.Buffered(1)), tab_spec, tab_spec, tab_spec],
        out_specs=[_row_spec(tm, ATTN_WIDTH), _row_spec(tm, KV_WIDTH), _row_spec(tm, KV_WIDTH),
                   _row_spec(tm, conv_width), _row_spec(tm, conv_width)],
        out_shape=[jax.ShapeDtypeStruct((rows, ATTN_WIDTH), BF16), jax.ShapeDtypeStruct((rows, KV_WIDTH), BF16),
                   jax.ShapeDtypeStruct((rows, KV_WIDTH), BF16), jax.ShapeDtypeStruct((rows, conv_width), BF16),
                   jax.ShapeDtypeStruct((rows, conv_width), BF16)],
        compiler_params=_cparams("parallel"),
        name="inproj_even",
    )(xs, mods, w_in, cos, sup, sdn)


def _attn_kernel(sink_ref, q_ref, kp_ref, kc_ref, kn_ref, kx_ref, vp_ref, vc_ref, vn_ref, vx_ref, o_ref, *,
                 nb, n_ctx, j):
    i = pl.program_id(1)
    blk = ATTN_BLOCK
    nkeys = n_ctx + 3 * blk
    rows = Q_PER_KV * blk
    latent = i < nb
    has_prev = jnp.logical_and(latent, i > 0)
    has_next = jnp.logical_and(latent, i < nb - 1)
    col = lax.broadcasted_iota(I32, (1, nkeys), 1)
    always, never = 2 * blk, -1
    hi = jnp.where(col < n_ctx, always,
                   jnp.where(col < n_ctx + blk, jnp.where(has_prev, col - n_ctx, never),
                             jnp.where(col < n_ctx + 2 * blk, jnp.where(latent, always, never),
                                       jnp.where(has_next, always, never))))
    lo = jnp.where(col >= n_ctx + 2 * blk, col - (n_ctx + 2 * blk), 0)
    qi = lax.broadcasted_iota(I32, (rows, nkeys), 0) & (blk - 1)
    bias = jnp.where(jnp.logical_and(qi <= hi, qi >= lo), 0.0, NEG_INF).astype(F32)
    grp = lax.broadcasted_iota(I32, (rows, 1), 0) // blk
    scale = HEAD_DIM ** -0.5
    for hk in range(N_KV_HEADS):
        sl = slice(hk * HEAD_DIM, (hk + 1) * HEAD_DIM)
        kcat = jnp.concatenate([kx_ref[:, sl], kp_ref[:, sl], kc_ref[:, sl], kn_ref[:, sl]], axis=0)
        vcat = jnp.concatenate([vx_ref[:, sl], vp_ref[:, sl], vc_ref[:, sl], vn_ref[:, sl]], axis=0)
        heads = [hk * Q_PER_KV + g for g in range(Q_PER_KV)]
        qs = jnp.concatenate([q_ref[:, hq * HEAD_DIM:(hq + 1) * HEAD_DIM] for hq in heads], axis=0)
        s = lax.dot_general(qs, kcat, (((1,), (1,)), ((), ())), preferred_element_type=F32) * scale + bias
        sink = jnp.full((rows, 1), sink_ref[j, heads[-1]], F32)
        for g in range(Q_PER_KV - 1):
            sink = jnp.where(grp == g, sink_ref[j, heads[g]], sink)
        m = jnp.maximum(jnp.max(s, axis=-1, keepdims=True), sink)
        p = jnp.exp(s - m)
        den = jnp.sum(p, axis=-1, keepdims=True) + jnp.exp(sink - m)
        pn = (p * (1.0 / den)).astype(BF16)
        o = jnp.dot(pn, vcat, preferred_element_type=F32)
        for g, hq in enumerate(heads):
            o_ref[:, hq * HEAD_DIM:(hq + 1) * HEAD_DIM] = o[g * blk:(g + 1) * blk].astype(BF16)


def _attention(q, k, v, sink, dims, ctx_out, j):
    blk = ATTN_BLOCK
    nb = dims.n // blk
    nqc = dims.n_ctx // blk if ctx_out else 0
    rows_out = dims.r_all if ctx_out else dims.r_lat
    lat_blocks = dims.r_lat // blk

    def q_map(b, i, *_):
        return (jnp.where(i < nb, b * nb + i, lat_blocks + b * nqc + (i - nb)), 0)

    def k_map(shift):
        return lambda b, i, *_: (b * nb + jnp.clip(i + shift, 0, nb - 1), 0)

    def ctx_map(b, i, *_):
        return (dims.r_lat // dims.n_ctx + b, 0)

    kv_specs = [pl.BlockSpec((blk, KV_WIDTH), k_map(-1)), pl.BlockSpec((blk, KV_WIDTH), k_map(0)),
                pl.BlockSpec((blk, KV_WIDTH), k_map(1)), pl.BlockSpec((dims.n_ctx, KV_WIDTH), ctx_map)]
    return pl.pallas_call(
        functools.partial(_attn_kernel, nb=nb, n_ctx=dims.n_ctx, j=j),
        grid=(dims.bsz, nb + nqc),
        in_specs=[pl.BlockSpec(memory_space=pltpu.SMEM), pl.BlockSpec((blk, ATTN_WIDTH), q_map)]
        + kv_specs + kv_specs,
        out_specs=pl.BlockSpec((blk, ATTN_WIDTH), q_map),
        out_shape=jax.ShapeDtypeStruct((rows_out, ATTN_WIDTH), BF16),
        compiler_params=_cparams("parallel", "parallel"),
        name="band_attention",
    )(sink, q, k, k, k, k, v, v, v, v)


def _halo_specs(tm, width, total_rows):
    per = tm // HALO_ROWS
    last = total_rows // HALO_ROWS - 1
    prev = pl.BlockSpec((HALO_ROWS, width), lambda i, *_: (jnp.maximum(i * per - 1, 0), 0))
    nxt = pl.BlockSpec((HALO_ROWS, width), lambda i, *_: (jnp.minimum((i + 1) * per, last), 0))
    return prev, nxt


def _outproj_even_kernel(a_ref, z_ref, zp_ref, zn_ref, gb_ref, cw_ref, w_ref, x_ref, mod_ref, g_ref, b_ref,
                         o_ref, cat_ref, *, dims, alpha):
    tm = x_ref.shape[0]
    row = lax.broadcasted_iota(I32, (tm, 1), 0)
    pos, length = _seq_position(pl.program_id(0) * tm + row, dims)
    z = z_ref[...].astype(F32)
    zprev = jnp.where(row == 0, zp_ref[HALO_ROWS - 1:HALO_ROWS, :].astype(F32), pltpu.roll(z, 1, 0))
    zprev = jnp.where(pos == 0, 0.0, zprev)
    znext = jnp.where(row == tm - 1, zn_ref[0:1, :].astype(F32), pltpu.roll(z, tm - 1, 0))
    znext = jnp.where(pos == length - 1, 0.0, znext)
    y = zprev * cw_ref[0:1, :] + z * cw_ref[1:2, :] + znext * cw_ref[2:3, :]
    cat_ref[:, 0:ATTN_WIDTH] = a_ref[...]
    cat_ref[:, ATTN_WIDTH:] = (gb_ref[...].astype(F32) * y).astype(BF16)
    mix = jnp.dot(cat_ref[...], w_ref[...], preferred_element_type=F32)
    o_ref[...] = _residual_ln(x_ref[...], mix, mod_ref[0, 2:3, :], g_ref[...], b_ref[...], alpha)


def _outproj_even(a, z, gb, conv_w, w_out, xs, mods, ln_g, ln_b, dims, rows, alpha, layer, j):
    d = dims.d
    conv_width = z.shape[1]
    tm = _pick_tile(512, dims.n, dims.r_ctx)
    zp_spec, zn_spec = _halo_specs(tm, conv_width, z.shape[0])
    return pl.pallas_call(
        functools.partial(_outproj_even_kernel, dims=dims, alpha=alpha),
        grid=(rows // tm,),
        in_specs=[_row_spec(tm, ATTN_WIDTH), _row_spec(tm, conv_width), zp_spec, zn_spec,
                  _row_spec(tm, conv_width), _layer_spec(conv_w, j), _layer_spec(w_out, layer),
                  _row_spec(tm, d), _mod_spec(dims, tm, layer), _layer_spec(ln_g, 2 * layer),
                  _layer_spec(ln_b, 2 * layer)],
        out_specs=_row_spec(tm, d),
        out_shape=jax.ShapeDtypeStruct((rows, d), F32),
        scratch_shapes=[pltpu.VMEM((tm, d), BF16)],
        compiler_params=_cparams("parallel"),
        name="outproj_even",
    )(a, z, z, z, gb, conv_w, w_out, xs, mods, ln_g, ln_b)


def _ffn_kernel(x_ref, mod_ref, wg_ref, wu_ref, wd_ref, g_ref, b_ref, o_ref, h_ref, acc_ref, *, alpha):
    k = pl.program_id(1)

    @pl.when(k == 0)
    def _():
        h_ref[...] = (x_ref[...] * (1.0 + mod_ref[0, 4:5, :]) + mod_ref[0, 3:4, :]).astype(BF16)
        acc_ref[...] = jnp.zeros_like(acc_ref)

    h = h_ref[...]
    gate = jnp.dot(h, wg_ref[...], preferred_element_type=F32)
    up = jnp.dot(h, wu_ref[...], preferred_element_type=F32)
    act = (_silu(gate) * up).astype(BF16)
    acc_ref[...] += jnp.dot(act, wd_ref[...], preferred_element_type=F32)

    @pl.when(k == pl.num_programs(1) - 1)
    def _():
        o_ref[...] = _residual_ln(x_ref[...], acc_ref[...], mod_ref[0, 5:6, :], g_ref[...], b_ref[...], alpha)


def _ffn(xs, mods, wg, wu, wd, ln_g, ln_b, dims, rows, alpha, layer, j):
    _, d, dff = wg.shape
    tm = _pick_tile(512, dims.n, dims.r_ctx)
    tf = _pick_tile(512, dff)
    return pl.pallas_call(
        functools.partial(_ffn_kernel, alpha=alpha),
        grid=(rows // tm, dff // tf),
        in_specs=[pl.BlockSpec((tm, d), lambda i, k: (i, 0)), _mod_spec(dims, tm, layer),
                  pl.BlockSpec((None, d, tf), lambda i, k: (j, 0, k)),
                  pl.BlockSpec((None, d, tf), lambda i, k: (j, 0, k)),
                  pl.BlockSpec((None, tf, d), lambda i, k: (j, k, 0)),
                  _layer_spec(ln_g, 2 * layer + 1), _layer_spec(ln_b, 2 * layer + 1)],
        out_specs=pl.BlockSpec((tm, d), lambda i, k: (i, 0)),
        out_shape=jax.ShapeDtypeStruct((rows, d), F32),
        scratch_shapes=[pltpu.VMEM((tm, d), BF16), pltpu.VMEM((tm, d), F32)],
        compiler_params=_cparams("parallel", "arbitrary"),
        name="dense_ffn",
    )(xs, mods, wg, wu, wd, ln_g, ln_b)


def _inproj_odd_kernel(x_ref, mod_ref, w_ref, up_ref, uf_ref):
    h = (x_ref[...] * (1.0 + mod_ref[0, 1:2, :]) + mod_ref[0, 0:1, :]).astype(BF16)
    pw = up_ref.shape[1]
    up_ref[...] = jnp.dot(h, w_ref[:, 0:pw], preferred_element_type=F32).astype(BF16)
    uf_ref[...] = jnp.dot(h, w_ref[:, pw:], preferred_element_type=F32).astype(BF16)


def _inproj_odd(xs, mods, w_in, dims, rows, pool_width, layer, j):
    d = dims.d
    four_width = w_in.shape[2] - pool_width
    tm = _pick_tile(512, dims.n, dims.r_ctx)
    return pl.pallas_call(
        _inproj_odd_kernel,
        grid=(rows // tm,),
        in_specs=[_row_spec(tm, d), _mod_spec(dims, tm, layer), _layer_spec(w_in, j)],
        out_specs=[_row_spec(tm, pool_width), _row_spec(tm, four_width)],
        out_shape=[jax.ShapeDtypeStruct((rows, pool_width), BF16), jax.ShapeDtypeStruct((rows, four_width), BF16)],
        compiler_params=_cparams("parallel"),
        name="inproj_odd",
    )(xs, mods, w_in)


def _seq_dft_constants(n, group):
    n2 = n // FFT_RADIX
    b = np.arange(n2, dtype=np.float64)
    ka = np.arange(FFT_RADIX, dtype=np.float64)
    ang = 2.0 * np.pi * np.outer(b, ka) / n
    tw = np.concatenate([np.cos(ang), -np.sin(ang)], axis=1)
    ang2 = 2.0 * np.pi * np.outer(b, b) / n2
    cs, sn = np.cos(ang2), np.sin(ang2)
    mat = np.block([[cs, sn], [-sn, cs]]) / np.sqrt(float(n) * group)
    return jnp.asarray(tw, F32), jnp.asarray(mat, BF16)


def _seq_dft_kernel(x_ref, tw_ref, m_ref, or_ref, oi_ref):
    n2 = x_ref.shape[1]
    lanes = x_ref.shape[2]
    for ka in range(FFT_RADIX):
        yr = yi = None
        for a in range(FFT_RADIX):
            ang = 2.0 * np.pi * ((a * ka) % FFT_RADIX) / FFT_RADIX
            cr, ci = float(np.round(np.cos(ang), 12)), float(np.round(-np.sin(ang), 12))
            xa = x_ref[a].astype(F32)
            if cr != 0.0:
                yr = cr * xa if yr is None else yr + cr * xa
            if ci != 0.0:
                yi = ci * xa if yi is None else yi + ci * xa
        if ka == 0:
            zr, zi = yr, jnp.zeros_like(yr)
        else:
            twr = jnp.broadcast_to(tw_ref[:, ka:ka + 1], (n2, lanes))
            twi = jnp.broadcast_to(tw_ref[:, FFT_RADIX + ka:FFT_RADIX + ka + 1], (n2, lanes))
            if yi is None:
                zr, zi = yr * twr, yr * twi
            else:
                zr, zi = yr * twr - yi * twi, yr * twi + yi * twr
        zcat = jnp.concatenate([zr, zi], axis=0).astype(BF16)
        res = jnp.dot(m_ref[...], zcat, preferred_element_type=F32)
        or_ref[ka] = res[0:n2].astype(BF16)
        oi_ref[ka] = res[n2:].astype(BF16)


def _seq_dft(uf, row0, bsz, n, group):
    rows, width = uf.shape
    n2 = n // FFT_RADIX
    lanes = _pick_tile(FFT_LANES, width)
    tw, mat = _seq_dft_constants(n, group)
    x3 = uf.reshape(rows // n2, n2, width)
    slab0 = row0 // n
    out = jax.ShapeDtypeStruct((bsz, FFT_RADIX, n2, width), BF16)
    ospec = pl.BlockSpec((None, FFT_RADIX, n2, lanes), lambda b, c: (b, 0, 0, c))
    return pl.pallas_call(
        _seq_dft_kernel,
        grid=(bsz, width // lanes),
        in_specs=[pl.BlockSpec((FFT_RADIX, n2, lanes), lambda b, c: (slab0 + b, 0, c)),
                  pl.BlockSpec(tw.shape, lambda b, c: (0, 0)),
                  pl.BlockSpec(mat.shape, lambda b, c: (0, 0), pipeline_mode=pl.Buffered(1))],
        out_specs=[ospec, ospec],
        out_shape=[out, out],
        compiler_params=_cparams("parallel", "parallel"),
        name="seq_dft",
    )(x3, tw, mat)


def _top2_routing(logits, n_exp):
    lane = lax.broadcasted_iota(I32, logits.shape, 1)
    lg = jnp.where(lane < n_exp, logits, -jnp.inf)
    m1 = jnp.max(lg, axis=-1, keepdims=True)
    i1 = jnp.min(jnp.where(lg == m1, lane, ROUTE_LANES), axis=-1, keepdims=True)
    lg2 = jnp.where(lane == i1, -jnp.inf, lg)
    m2 = jnp.max(lg2, axis=-1, keepdims=True)
    i2 = jnp.min(jnp.where(lg2 == m2, lane, ROUTE_LANES), axis=-1, keepdims=True)
    e2 = jnp.exp(m2 - m1)
    g1 = 1.0 / (1.0 + e2)
    g2 = e2 * g1
    return jnp.where(lane == 0, i1.astype(F32),
                     jnp.where(lane == 1, i2.astype(F32), jnp.where(lane == 2, g1, jnp.where(lane == 3, g2, 0.0))))


def _outproj_odd_kernel(*refs, dims, alpha, has_ctx, n_exp):
    (up_ref, upp_ref, upn_ref, xr_ref, xi_ref) = refs[:5]
    refs = refs[5:]
    if has_ctx:
        (cr_ref, ci_ref) = refs[:2]
        refs = refs[2:]
    (perm_ref, cs_ref, fw_ref, pw_ref, ps_ref, w_ref, x_ref, mod_ref, g_ref, b_ref, rwh_ref, rwl_ref,
     o_ref, h2_ref, route_ref, ext_ref, cat_ref) = refs
    tm = x_ref.shape[0]
    i = pl.program_id(0)
    row = lax.broadcasted_iota(I32, (tm, 1), 0)
    pos, length = _seq_position(i * tm + row, dims)
    first_pos, _ = _seq_position(i * tm, dims)
    last_pos, last_len = _seq_position(i * tm + tm - 1, dims)

    pool_width = up_ref.shape[1]
    group = pool_width // len(POOL_WINDOWS)
    h0 = POOL_HALO
    ext_ref[0:h0, :] = jnp.where(first_pos == 0, 0.0, upp_ref[HALO_ROWS - h0:HALO_ROWS, :].astype(F32))
    ext_ref[h0:h0 + tm, :] = up_ref[...].astype(F32)
    ext_ref[h0 + tm:h0 + tm + h0, :] = jnp.where(last_pos == last_len - 1, 0.0, upn_ref[0:h0, :].astype(F32))
    for gi, w in enumerate(POOL_WINDOWS):
        cols = slice(gi * group, (gi + 1) * group)
        back, ahead = w // 2, w - w // 2 - 1
        tot = None
        for s in range(-back, ahead + 1):
            t = ext_ref[h0 + s:h0 + s + tm, cols]
            tot = t if tot is None else tot + t
        cnt = jnp.minimum(pos + ahead, length - 1) - jnp.maximum(pos - back, 0) + 1
        pooled = tot / cnt.astype(F32) - ext_ref[h0:h0 + tm, cols]
        yp = jnp.dot(pooled.astype(BF16), pw_ref[gi], preferred_element_type=F32) * ps_ref[:, cols]
        cat_ref[:, cols] = yp.astype(BF16)

    if has_ctx:
        lat = i * tm < dims.r_lat
        xr = jnp.where(lat, xr_ref[...], cr_ref[...])
        xi = jnp.where(lat, xi_ref[...], ci_ref[...])
    else:
        xr, xi = xr_ref[...], xi_ref[...]
    four_width = xr.shape[-1]
    fgroup = four_width // FOURIER_HEADS
    xr = jnp.dot(perm_ref[...], xr.reshape(tm, four_width), preferred_element_type=F32).astype(BF16)
    xi = jnp.dot(perm_ref[...], xi.reshape(tm, four_width), preferred_element_type=F32).astype(BF16)
    for hd in range(FOURIER_HEADS):
        cols = slice(hd * fgroup, (hd + 1) * fgroup)
        both = jnp.concatenate([xr[:, cols], xi[:, cols]], axis=1)
        f = jnp.dot(both, cs_ref[...], preferred_element_type=F32)
        yf = jnp.dot(f.astype(BF16), fw_ref[hd], preferred_element_type=F32)
        cat_ref[:, pool_width + hd * fgroup:pool_width + (hd + 1) * fgroup] = yf.astype(BF16)

    mix = jnp.dot(cat_ref[...], w_ref[...], preferred_element_type=F32)
    xn = _residual_ln(x_ref[...], mix, mod_ref[0, 2:3, :], g_ref[...], b_ref[...], alpha)
    o_ref[...] = xn
    h2 = xn * (1.0 + mod_ref[0, 4:5, :]) + mod_ref[0, 3:4, :]
    h2_ref[...] = h2
    h_hi = h2.astype(BF16)
    h_lo = (h2 - h_hi.astype(F32)).astype(BF16)
    logits = (jnp.dot(h_hi, rwh_ref[...], preferred_element_type=F32)
              + (jnp.dot(h_lo, rwh_ref[...], preferred_element_type=F32)
                 + jnp.dot(h_hi, rwl_ref[...], preferred_element_type=F32)))
    route_ref[...] = _top2_routing(logits, n_exp)


def _channel_dft_matrix(group):
    c = np.arange(group, dtype=np.float64)
    ang = 2.0 * np.pi * np.outer(c, c) / group
    return jnp.asarray(np.concatenate([np.cos(ang), np.sin(ang)], axis=0), BF16)


def _tile_permutation(tm):
    per = tm // FFT_RADIX
    p = np.zeros((tm, tm), np.float32)
    for ka in range(FFT_RADIX):
        for kb in range(per):
            p[kb * FFT_RADIX + ka, ka * per + kb] = 1.0
    return jnp.asarray(p, BF16)


def _outproj_odd(up, lat_ri, ctx_ri, fourier_w, pool_w, pool_scale, w_out, xs, mods, ln_g, ln_b, rw_hi, rw_lo,
                 n_exp, dims, rows, alpha, layer, j):
    d = dims.d
    tm = SEQ_TILE
    pool_width = up.shape[1]
    four_width = lat_ri[0].shape[-1]
    fgroup = four_width // FOURIER_HEADS
    has_ctx = ctx_ri is not None
    per = tm // FFT_RADIX
    lat_tiles = dims.n // tm
    n_lat = dims.r_lat // tm
    upp_spec, upn_spec = _halo_specs(tm, pool_width, up.shape[0])

    def lat_map(i):
        t = jnp.minimum(i, n_lat - 1)
        return (t // lat_tiles, 0, t % lat_tiles, 0)

    lat_spec = pl.BlockSpec((None, FFT_RADIX, per, four_width), lat_map)
    operands = [up, up, up, lat_ri[0], lat_ri[1]]
    specs = [_row_spec(tm, pool_width), upp_spec, upn_spec, lat_spec, lat_spec]
    if has_ctx:
        ctx_tiles = dims.n_ctx // tm

        def ctx_map(i):
            t = jnp.maximum(i - n_lat, 0)
            return (t // ctx_tiles, 0, t % ctx_tiles, 0)

        ctx_spec = pl.BlockSpec((None, FFT_RADIX, per, four_width), ctx_map)
        operands += [ctx_ri[0], ctx_ri[1]]
        specs += [ctx_spec, ctx_spec]
    perm, cs = _tile_permutation(tm), _channel_dft_matrix(fgroup)
    operands += [perm, cs, fourier_w, pool_w, pool_scale, w_out, xs, mods, ln_g, ln_b, rw_hi, rw_lo]
    specs += [_const_spec(perm.shape), _const_spec(cs.shape), _layer_spec(fourier_w, j), _layer_spec(pool_w, j),
              _layer_spec(pool_scale, j), _layer_spec(w_out, layer), _row_spec(tm, d),
              _mod_spec(dims, tm, layer), _layer_spec(ln_g, 2 * layer), _layer_spec(ln_b, 2 * layer),
              _layer_spec(rw_hi, j), _layer_spec(rw_lo, j)]
    return pl.pallas_call(
        functools.partial(_outproj_odd_kernel, dims=dims, alpha=alpha, has_ctx=has_ctx, n_exp=n_exp),
        grid=(rows // tm,),
        in_specs=specs,
        out_specs=[_row_spec(tm, d), _row_spec(tm, d), _row_spec(tm, ROUTE_LANES)],
        out_shape=[jax.ShapeDtypeStruct((rows, d), F32), jax.ShapeDtypeStruct((rows, d), F32),
                   jax.ShapeDtypeStruct((rows, ROUTE_LANES), F32)],
        scratch_shapes=[pltpu.VMEM((tm + 2 * POOL_HALO, pool_width), F32), pltpu.VMEM((tm, d), BF16)],
        compiler_params=_cparams("parallel"),
        name="outproj_odd",
    )(*operands)


def _dispatch_plan(route, n_exp, tile):
    tokens = route.shape[0]
    assign = TOP_K * tokens
    experts = route[:, 0:TOP_K].astype(I32).reshape(assign)
    onehot = (experts[:, None] == jnp.arange(n_exp, dtype=I32)[None, :]).astype(I32)
    running = jnp.cumsum(onehot, axis=0)
    rank = jnp.sum((running - onehot) * onehot, axis=1)
    counts = running[-1]
    padded = ((counts + tile - 1) // tile) * tile
    ends = jnp.cumsum(padded)
    starts = ends - padded
    slot = jnp.sum(onehot * starts[None, :], axis=1) + rank
    n_tiles = -(-(assign + n_exp * (tile - 1)) // tile)
    tile_start = jnp.arange(n_tiles, dtype=I32) * tile
    tile_expert = jnp.minimum(jnp.sum((tile_start[:, None] >= ends[None, :]).astype(I32), axis=1), n_exp - 1)
    live_tiles = ends[-1] // tile
    tile_expert = jnp.where(tile_start < ends[-1], tile_expert, tile_expert[jnp.maximum(live_tiles - 1, 0)])
    pad_first = starts + counts
    pad_head = jnp.minimum((-pad_first) % SUBLANES, padded - counts)
    pad_info = jnp.concatenate([pad_first, pad_head, pad_first + pad_head, padded - counts - pad_head,
                                live_tiles[None]]).astype(I32)
    return slot, pad_info, tile_expert, n_tiles


def _dispatch_kernel(slot_ref, pad_ref, h_ref, o_ref, zero_ref, sem, zsem, *, n_exp, tile, n_tiles, min_tiles):
    i = pl.program_id(0)
    tm = h_ref.shape[0]

    def zero_copy(first_row, size):
        return pltpu.make_async_copy(zero_ref.at[pl.ds(0, size)], o_ref.at[pl.ds(first_row, size)], zsem)

    def padding_copies(act):
        for e in range(n_exp):
            first, head = pad_ref[e], pad_ref[n_exp + e]
            for r in range(SUBLANES - 1):
                pl.when(r < head)(functools.partial(act, zero_copy(first + r, 1)))
            first, length = pad_ref[2 * n_exp + e], pad_ref[3 * n_exp + e]
            for bit in reversed(range(SUBLANES.bit_length() - 1, tile.bit_length() - 1)):
                size = 1 << bit
                for part in range(-(-size // ZERO_ROWS)):
                    rows = min(size, ZERO_ROWS)
                    at = pl.multiple_of(first + part * rows, SUBLANES)
                    pl.when((length & size) != 0)(functools.partial(act, zero_copy(at, rows)))
                first = first + (length & size)
        live = pad_ref[PAD_LIVE * n_exp]
        for t in range(min_tiles, n_tiles):
            for part in range(tile // ZERO_ROWS):
                pl.when(t >= live)(functools.partial(act, zero_copy(t * tile + part * ZERO_ROWS, ZERO_ROWS)))

    @pl.when(i == 0)
    def _():
        zero_ref[...] = jnp.zeros_like(zero_ref)
        padding_copies(lambda cp: cp.start())
        padding_copies(lambda cp: cp.wait())

    base = i * tm * TOP_K

    def issue(r, carry):
        for c in range(TOP_K):
            dst = slot_ref[base + r * TOP_K + c]
            pltpu.make_async_copy(h_ref.at[pl.ds(r, 1)], o_ref.at[pl.ds(dst, 1)], sem).start()
        return carry

    lax.fori_loop(0, tm, issue, 0)
    for c in range(TOP_K):
        pltpu.make_async_copy(h_ref, o_ref.at[pl.ds(0, tm)], sem).wait()


def _dispatch_rows(slot, pad_info, h2, n_exp, tile, n_tiles):
    tokens, d = h2.shape
    tm = SEQ_TILE
    min_tiles = -(-(TOP_K * tokens) // tile)
    return pl.pallas_call(
        functools.partial(_dispatch_kernel, n_exp=n_exp, tile=tile, n_tiles=n_tiles, min_tiles=min_tiles),
        grid_spec=pltpu.PrefetchScalarGridSpec(
            num_scalar_prefetch=2,
            grid=(tokens // tm,),
            in_specs=[_row_spec(tm, d)],
            out_specs=pl.BlockSpec(memory_space=pl.ANY),
            scratch_shapes=[pltpu.VMEM((ZERO_ROWS, d), h2.dtype), pltpu.SemaphoreType.DMA(()),
                            pltpu.SemaphoreType.DMA(())]),
        out_shape=jax.ShapeDtypeStruct((n_tiles * tile, d), h2.dtype),
        compiler_params=_cparams("arbitrary", unchecked=True),
        name="moe_dispatch",
    )(slot, pad_info, h2)


def _expert_kernel(te_ref, pad_ref, x_ref, wg_ref, wu_ref, wd_ref, o_ref, h_ref, acc_ref, *, n_exp):
    i, k = pl.program_id(0), pl.program_id(1)
    last = pl.num_programs(1) - 1
    live = i < pad_ref[PAD_LIVE * n_exp]

    @pl.when(jnp.logical_and(live, k == 0))
    def _():
        h_ref[...] = x_ref[...].astype(BF16)
        acc_ref[...] = jnp.zeros_like(acc_ref)

    @pl.when(live)
    def _():
        h = h_ref[...]
        gate = jnp.dot(h, wg_ref[...], preferred_element_type=F32)
        up = jnp.dot(h, wu_ref[...], preferred_element_type=F32)
        act = (_silu(gate) * up).astype(BF16)
        acc_ref[...] += jnp.dot(act, wd_ref[...], preferred_element_type=F32)

    @pl.when(jnp.logical_and(live, k == last))
    def _():
        o_ref[...] = acc_ref[...]

    @pl.when(jnp.logical_and(jnp.logical_not(live), k == last))
    def _():
        o_ref[...] = jnp.zeros_like(o_ref)


def _expert_ffn(tile_expert, pad_info, xs, wg, wu, wd, j):
    n_slots, d = xs.shape
    _, n_exp, _, dff = wg.shape
    tm = EXPERT_TILE
    tf = _pick_tile(512, dff)
    nk = dff // tf

    def chunk(i, k, pad):
        return jnp.where(i < pad[PAD_LIVE * n_exp], k, nk - 1)

    return pl.pallas_call(
        functools.partial(_expert_kernel, n_exp=n_exp),
        grid_spec=pltpu.PrefetchScalarGridSpec(
            num_scalar_prefetch=2,
            grid=(n_slots // tm, nk),
            in_specs=[pl.BlockSpec((tm, d), lambda i, k, te, pad: (i, 0)),
                      pl.BlockSpec((None, None, d, tf), lambda i, k, te, pad: (j, te[i], 0, chunk(i, k, pad))),
                      pl.BlockSpec((None, None, d, tf), lambda i, k, te, pad: (j, te[i], 0, chunk(i, k, pad))),
                      pl.BlockSpec((None, None, tf, d), lambda i, k, te, pad: (j, te[i], chunk(i, k, pad), 0))],
            out_specs=pl.BlockSpec((tm, d), lambda i, k, te, pad: (i, 0)),
            scratch_shapes=[pltpu.VMEM((tm, d), BF16), pltpu.VMEM((tm, d), F32)]),
        out_shape=jax.ShapeDtypeStruct((n_slots, d), F32),
        compiler_params=_cparams("arbitrary", "arbitrary"),
        name="expert_ffn",
    )(tile_expert, pad_info, xs, wg, wu, wd)


def _combine_kernel(slot_ref, y_ref, route_ref, x_ref, mod_ref, g_ref, b_ref, o_ref, buf_ref, sem, *, alpha):
    tm = x_ref.shape[0]
    base = pl.program_id(0) * tm * TOP_K

    def issue(r, carry):
        for c in range(TOP_K):
            src = slot_ref[base + r * TOP_K + c]
            pltpu.make_async_copy(y_ref.at[pl.ds(src, 1)], buf_ref.at[c, pl.ds(r, 1)], sem).start()
        return carry

    lax.fori_loop(0, tm, issue, 0)
    for c in range(TOP_K):
        pltpu.make_async_copy(y_ref.at[pl.ds(0, tm)], buf_ref.at[c], sem).wait()
    route = route_ref[...]
    ff = route[:, 2:3] * buf_ref[0] + route[:, 3:4] * buf_ref[1]
    o_ref[...] = _residual_ln(x_ref[...], ff, mod_ref[0, 5:6, :], g_ref[...], b_ref[...], alpha)


def _combine(slot, ys, route, xs, mods, ln_g, ln_b, dims, rows, alpha, layer):
    d = dims.d
    tm = SEQ_TILE
    return pl.pallas_call(
        functools.partial(_combine_kernel, alpha=alpha),
        grid_spec=pltpu.PrefetchScalarGridSpec(
            num_scalar_prefetch=1,
            grid=(rows // tm,),
            in_specs=[pl.BlockSpec(memory_space=pl.ANY), _row_spec(tm, ROUTE_LANES), _row_spec(tm, d),
                      _mod_spec(dims, tm, layer), _layer_spec(ln_g, 2 * layer + 1),
                      _layer_spec(ln_b, 2 * layer + 1)],
            out_specs=_row_spec(tm, d),
            scratch_shapes=[pltpu.VMEM((TOP_K, tm, d), F32), pltpu.SemaphoreType.DMA(())]),
        out_shape=jax.ShapeDtypeStruct((rows, d), F32),
        compiler_params=_cparams("arbitrary", unchecked=True),
        name="moe_combine",
    )(slot, ys, route, xs, mods, ln_g, ln_b)


def _moe(xs, h2, route, mods, wg, wu, wd, ln_g, ln_b, dims, rows, alpha, layer, j):
    n_exp = wg.shape[1]
    slot, pad_info, tile_expert, n_tiles = _dispatch_plan(route, n_exp, EXPERT_TILE)
    sorted_rows = _dispatch_rows(slot, pad_info, h2, n_exp, EXPERT_TILE, n_tiles)
    ys = _expert_ffn(tile_expert, pad_info, sorted_rows, wg, wu, wd, j)
    return _combine(slot, ys, route, xs, mods, ln_g, ln_b, dims, rows, alpha, layer)


def kernel(x, c, ctx, c_ctx, w_mod, b_mod, w_mix_out, ln_g, ln_b, w_in_ab, conv_w, attn_sink, w_in_cd, pool_w,
           pool_scale, fourier_w, ffn_w_gate, ffn_w_up, ffn_w_down, router_w, moe_w_gate, moe_w_up, moe_w_down):
    bsz, n, d = x.shape
    n_ctx = ctx.shape[1]
    depth = w_mod.shape[0]
    n_exp = router_w.shape[2]
    dims = Dims(bsz, n, n_ctx, d)
    assert n % SEQ_TILE == 0 and n_ctx % SEQ_TILE == 0 and n % GRID_W == 0
    assert dims.r_lat % (FFT_RADIX * (n_ctx // FFT_RADIX)) == 0 and dims.r_all % (n // FFT_RADIX) == 0
    alpha = (2.0 * depth) ** 0.25
    pool_width = pool_w.shape[1] * pool_w.shape[2]
    fgroup = fourier_w.shape[2]

    mods = _modulations(c, c_ctx, w_mod, b_mod)
    w_out = w_mix_out.astype(BF16)
    w_ab, w_cd = w_in_ab.astype(BF16), w_in_cd.astype(BF16)
    ffn_g, ffn_u, ffn_d = ffn_w_gate.astype(BF16), ffn_w_up.astype(BF16), ffn_w_down.astype(BF16)
    moe_g, moe_u, moe_d = moe_w_gate.astype(BF16), moe_w_up.astype(BF16), moe_w_down.astype(BF16)
    four_w, pool_wb = fourier_w.astype(BF16), pool_w.astype(BF16)
    pool_sc = pool_scale[:, None, :]
    lng, lnb = ln_g.reshape(2 * depth, 1, d), ln_b.reshape(2 * depth, 1, d)
    rw = jnp.zeros(router_w.shape[:2] + (ROUTE_LANES,), F32).at[:, :, :n_exp].set(router_w)
    rw_hi = rw.astype(BF16)
    rw_lo = (rw - rw_hi.astype(F32)).astype(BF16)

    xs = jnp.concatenate([x.reshape(dims.r_lat, d), ctx.reshape(dims.r_ctx, d)], axis=0)
    for l in range(depth):
        even = l % 2 == 0
        j = l // 2
        ctx_out = any(m % 2 == 0 for m in range(l + 1, depth))
        rows = dims.r_all if ctx_out else dims.r_lat
        if even:
            q, k, v, z, gb = _inproj_even(xs, mods, w_ab, dims, l, j)
            a = _attention(q, k, v, attn_sink, dims, ctx_out, j)
            xs = _outproj_even(a, z, gb, conv_w, w_out, xs, mods, lng, lnb, dims, rows, alpha, l, j)
            xs = _ffn(xs, mods, ffn_g, ffn_u, ffn_d, lng, lnb, dims, rows, alpha, l, j)
        else:
            up, uf = _inproj_odd(xs, mods, w_cd, dims, rows, pool_width, l, j)
            lat_ri = _seq_dft(uf, 0, bsz, n, fgroup)
            ctx_ri = _seq_dft(uf, dims.r_lat, bsz, n_ctx, fgroup) if ctx_out else None
            xs, h2, route = _outproj_odd(up, lat_ri, ctx_ri, four_w, pool_wb, pool_sc, w_out, xs, mods, lng, lnb,
                                         rw_hi, rw_lo, n_exp, dims, rows, alpha, l, j)
            xs = _moe(xs, h2, route, mods, moe_g, moe_u, moe_d, lng, lnb, dims, rows, alpha, l, j)
    return xs[:dims.r_lat].reshape(bsz, n, d)
```

```python
import functools
from typing import NamedTuple

import numpy as np
import jax
import jax.numpy as jnp
from jax import lax
from jax.experimental import pallas as pl
from jax.experimental.pallas import tpu as pltpu

F32 = jnp.float32
BF16 = jnp.bfloat16
I32 = jnp.int32

HEAD_DIM = 128
N_Q_HEADS = 8
N_KV_HEADS = 2
Q_PER_KV = N_Q_HEADS // N_KV_HEADS
ATTN_WIDTH = N_Q_HEADS * HEAD_DIM
KV_WIDTH = N_KV_HEADS * HEAD_DIM
ATTN_BLOCK = 128
GRID_W = 64
ROPE_THETA = 10000.0
NEG_INF = -1e30
POOL_WINDOWS = (2, 4, 8, 16)
POOL_HALO = 8
HALO_ROWS = 16
FOURIER_HEADS = 4
N_MOD = 6
LN_EPS = 1e-5
TOP_K = 2
FFT_RADIX = 8
FFT_LANES = 256
SEQ_TILE = 256
ROUTE_LANES = 128
EXPERT_TILE = 512
ZERO_ROWS = 256
SUBLANES = 8
PAD_LIVE = 4
V7X_VMEM_LIMIT = 56 * 1024 * 1024


class Dims(NamedTuple):
    bsz: int
    n: int
    n_ctx: int
    d: int

    @property
    def r_lat(self):
        return self.bsz * self.n

    @property
    def r_ctx(self):
        return self.bsz * self.n_ctx

    @property
    def r_all(self):
        return self.r_lat + self.r_ctx


def _pick_tile(pref, *dims):
    t = pref
    while any(d % t for d in dims):
        t //= 2
    return t


def _cparams(*sem, unchecked=False):
    return pltpu.CompilerParams(dimension_semantics=sem, vmem_limit_bytes=V7X_VMEM_LIMIT,
                                disable_bounds_checks=unchecked)


def _residual_ln(x, y, gate, g, b, alpha):
    v = alpha * x + gate * y
    mu = jnp.mean(v, axis=-1, keepdims=True)
    vc = v - mu
    var = jnp.mean(vc * vc, axis=-1, keepdims=True)
    return vc * lax.rsqrt(var + LN_EPS) * g + b


def _silu(t):
    return t * jax.nn.sigmoid(t)


def _seq_position(row, dims):
    lat = row < dims.r_lat
    pos = jnp.where(lat, lax.rem(row, dims.n), lax.rem(row - dims.r_lat, dims.n_ctx))
    length = jnp.where(lat, dims.n, dims.n_ctx)
    return pos, length


def _mod_spec(dims, tm, layer):
    return pl.BlockSpec((None, 1, N_MOD, dims.d), lambda i, *_: (layer, (i * tm) // dims.n, 0, 0))


def _row_spec(tm, width):
    return pl.BlockSpec((tm, width), lambda i, *_: (i, 0))


def _stream_specs(streams, tm, dims):
    if len(streams) == 1:
        return [_row_spec(tm, dims.d)]
    n_lat = dims.r_lat // tm
    return [pl.BlockSpec((tm, dims.d), lambda i, *_: (jnp.minimum(i, n_lat - 1), 0)),
            pl.BlockSpec((tm, dims.d), lambda i, *_: (jnp.maximum(i - n_lat, 0), 0))]


def _stream_rows(refs, dims):
    if len(refs) == 1:
        return refs[0][...]
    tm = refs[0].shape[0]
    return jnp.where(pl.program_id(0) * tm < dims.r_lat, refs[0][...], refs[1][...])


def _layer_spec(arr, index, **kw):
    zeros = (0,) * (arr.ndim - 1)
    return pl.BlockSpec((None,) + arr.shape[1:], lambda *_: (index,) + zeros, **kw)


def _const_spec(shape):
    zeros = (0,) * len(shape)
    return pl.BlockSpec(shape, lambda *_: zeros)


def _mod_kernel(cond_ref, w_ref, b_ref, o_ref):
    s = _silu(cond_ref[...]).astype(BF16)
    o_ref[...] = jnp.dot(s, w_ref[...].astype(BF16), preferred_element_type=F32) + b_ref[...]


def _modulations(c, c_ctx, w_mod, b_mod):
    depth, d, nmod = w_mod.shape
    bsz = c.shape[0]
    rows = -(-(bsz + 1) // 8) * 8
    cond = jnp.zeros((rows, d), F32).at[:bsz].set(c).at[bsz].set(c_ctx)
    tn = _pick_tile(1024, nmod)
    out = pl.pallas_call(
        _mod_kernel,
        grid=(depth, nmod // tn),
        in_specs=[pl.BlockSpec((rows, d), lambda l, j: (0, 0)),
                  pl.BlockSpec((None, d, tn), lambda l, j: (l, 0, j)),
                  pl.BlockSpec((None, 1, tn), lambda l, j: (l, 0, j))],
        out_specs=pl.BlockSpec((None, rows, tn), lambda l, j: (l, 0, j)),
        out_shape=jax.ShapeDtypeStruct((depth, rows, nmod), F32),
        compiler_params=_cparams("arbitrary", "arbitrary"),
        name="modulation",
    )(cond, w_mod, b_mod.reshape(depth, 1, nmod))
    return out.reshape(depth, rows, N_MOD, d)


def _rope_tables(n, tm):
    rows = n // GRID_W
    row = jnp.repeat(jnp.arange(rows, dtype=F32), GRID_W)
    col = jnp.tile(jnp.arange(GRID_W, dtype=F32), rows)
    half = HEAD_DIM // 2
    inv = ROPE_THETA ** (-jnp.arange(0, half, 2, dtype=F32) / half)
    ang_r = row[:, None] * inv
    ang_c = col[:, None] * inv
    ang = jnp.concatenate([ang_r, ang_r, ang_c, ang_c], -1)
    cos, sin = jnp.cos(ang), jnp.sin(ang)
    quarter = HEAD_DIM // 4
    first = (jnp.arange(HEAD_DIM) // quarter) % 2 == 0
    sin_up = jnp.where(first, -sin, 0.0)
    sin_dn = jnp.where(first, 0.0, sin)
    ident = jnp.zeros((tm, HEAD_DIM), F32)
    return (jnp.concatenate([cos, ident + 1.0], 0), jnp.concatenate([sin_up, ident], 0),
            jnp.concatenate([sin_dn, ident], 0))


def _inproj_even_kernel(*refs, conv_width, dims, n_streams):
    x_refs, refs = refs[:n_streams], refs[n_streams:]
    mod_ref, w_ref, cos_ref, sup_ref, sdn_ref, q_ref, k_ref, v_ref, z_ref, gb_ref = refs
    h = (_stream_rows(x_refs, dims) * (1.0 + mod_ref[0, 1:2, :]) + mod_ref[0, 0:1, :]).astype(BF16)
    cos, sup, sdn = cos_ref[...], sup_ref[...], sdn_ref[...]
    quarter = HEAD_DIM // 4

    def rope(u):
        return u * cos + pltpu.roll(u, HEAD_DIM - quarter, 1) * sup + pltpu.roll(u, quarter, 1) * sdn

    q = jnp.dot(h, w_ref[:, 0:ATTN_WIDTH], preferred_element_type=F32)
    for hd in range(N_Q_HEADS):
        sl = slice(hd * HEAD_DIM, (hd + 1) * HEAD_DIM)
        q_ref[:, sl] = rope(q[:, sl]).astype(BF16)
    o = ATTN_WIDTH
    kv = jnp.dot(h, w_ref[:, o:o + 2 * KV_WIDTH], preferred_element_type=F32)
    for hd in range(N_KV_HEADS):
        sl = slice(hd * HEAD_DIM, (hd + 1) * HEAD_DIM)
        k_ref[:, sl] = rope(kv[:, sl]).astype(BF16)
    v_ref[...] = kv[:, KV_WIDTH:].astype(BF16)
    o += 2 * KV_WIDTH
    u = jnp.dot(h, w_ref[:, o:o + conv_width], preferred_element_type=F32)
    gc = jnp.dot(h, w_ref[:, o + 2 * conv_width:o + 3 * conv_width], preferred_element_type=F32)
    z_ref[...] = (gc * u).astype(BF16)
    gb = jnp.dot(h, w_ref[:, o + conv_width:o + 2 * conv_width], preferred_element_type=F32)
    gb_ref[...] = gb.astype(BF16)


def _inproj_even(streams, mods, w_in, dims, layer, j):
    rows = dims.r_all
    conv_width = (w_in.shape[2] - ATTN_WIDTH - 2 * KV_WIDTH) // 3
    tm = _pick_tile(512, dims.n, dims.r_ctx)
    cos, sup, sdn = _rope_tables(dims.n, tm)
    nlat = dims.n // tm
    tab_spec = pl.BlockSpec((tm, HEAD_DIM), lambda i: (jnp.where(i * tm < dims.r_lat, i % nlat, nlat), 0))
    return pl.pallas_call(
        functools.partial(_inproj_even_kernel, conv_width=conv_width, dims=dims, n_streams=len(streams)),
        grid=(rows // tm,),
        in_specs=_stream_specs(streams, tm, dims)
        + [_mod_spec(dims, tm, layer), _layer_spec(w_in, j, pipeline_mode=pl.Buffered(1)),
           tab_spec, tab_spec, tab_spec],
        out_specs=[_row_spec(tm, ATTN_WIDTH), _row_spec(tm, KV_WIDTH), _row_spec(tm, KV_WIDTH),
                   _row_spec(tm, conv_width), _row_spec(tm, conv_width)],
        out_shape=[jax.ShapeDtypeStruct((rows, ATTN_WIDTH), BF16), jax.ShapeDtypeStruct((rows, KV_WIDTH), BF16),
                   jax.ShapeDtypeStruct((rows, KV_WIDTH), BF16), jax.ShapeDtypeStruct((rows, conv_width), BF16),
                   jax.ShapeDtypeStruct((rows, conv_width), BF16)],
        compiler_params=_cparams("parallel"),
        name="inproj_even",
    )(*streams, mods, w_in, cos, sup, sdn)


def _attn_kernel(sink_ref, q_ref, kp_ref, kc_ref, kn_ref, kx_ref, vp_ref, vc_ref, vn_ref, vx_ref, o_ref, *,
                 nb, n_ctx, j):
    i = pl.program_id(1)
    blk = ATTN_BLOCK
    rows = Q_PER_KV * blk
    latent = i < nb
    has_prev = jnp.logical_and(latent, i > 0)
    has_next = jnp.logical_and(latent, i < nb - 1)
    qi = lax.broadcasted_iota(I32, (blk, blk), 0)
    kj = lax.broadcasted_iota(I32, (blk, blk), 1)
    bias_prev = jnp.where(jnp.logical_and(kj >= qi, has_prev), 0.0, NEG_INF).astype(F32)
    bias_next = jnp.where(jnp.logical_and(kj <= qi, has_next), 0.0, NEG_INF).astype(F32)
    bias_prev = jnp.concatenate([bias_prev] * Q_PER_KV, axis=0)
    bias_next = jnp.concatenate([bias_next] * Q_PER_KV, axis=0)
    bias_own = jnp.where(latent, 0.0, NEG_INF).astype(F32)
    grp = lax.broadcasted_iota(I32, (rows, 1), 0) // blk
    scale = HEAD_DIM ** -0.5
    for hk in range(N_KV_HEADS):
        sl = slice(hk * HEAD_DIM, (hk + 1) * HEAD_DIM)
        kcat = jnp.concatenate([kx_ref[:, sl], kp_ref[:, sl], kc_ref[:, sl], kn_ref[:, sl]], axis=0)
        vcat = jnp.concatenate([vx_ref[:, sl], vp_ref[:, sl], vc_ref[:, sl], vn_ref[:, sl]], axis=0)
        heads = [hk * Q_PER_KV + g for g in range(Q_PER_KV)]
        qs = jnp.concatenate([q_ref[:, hq * HEAD_DIM:(hq + 1) * HEAD_DIM] for hq in heads], axis=0)
        s = lax.dot_general(qs, kcat, (((1,), (1,)), ((), ())), preferred_element_type=F32) * scale
        s = jnp.concatenate([s[:, 0:n_ctx], s[:, n_ctx:n_ctx + blk] + bias_prev,
                             s[:, n_ctx + blk:n_ctx + 2 * blk] + bias_own, s[:, n_ctx + 2 * blk:] + bias_next], axis=1)
        sink = jnp.full((rows, 1), sink_ref[j, heads[-1]], F32)
        for g in range(Q_PER_KV - 1):
            sink = jnp.where(grp == g, sink_ref[j, heads[g]], sink)
        m = jnp.maximum(jnp.max(s, axis=-1, keepdims=True), sink)
        p = jnp.exp(s - m)
        den = jnp.sum(p, axis=-1, keepdims=True) + jnp.exp(sink - m)
        pn = (p * (1.0 / den)).astype(BF16)
        o = jnp.dot(pn, vcat, preferred_element_type=F32)
        for g, hq in enumerate(heads):
            o_ref[:, hq * HEAD_DIM:(hq + 1) * HEAD_DIM] = o[g * blk:(g + 1) * blk].astype(BF16)


def _attention(q, k, v, sink, dims, ctx_out, j):
    blk = ATTN_BLOCK
    nb = dims.n // blk
    nqc = dims.n_ctx // blk if ctx_out else 0
    rows_out = dims.r_all if ctx_out else dims.r_lat
    lat_blocks = dims.r_lat // blk

    def q_map(b, i, *_):
        return (jnp.where(i < nb, b * nb + i, lat_blocks + b * nqc + (i - nb)), 0)

    def k_map(shift):
        return lambda b, i, *_: (b * nb + jnp.clip(i + shift, 0, nb - 1), 0)

    def ctx_map(b, i, *_):
        return (dims.r_lat // dims.n_ctx + b, 0)

    kv_specs = [pl.BlockSpec((blk, KV_WIDTH), k_map(-1)), pl.BlockSpec((blk, KV_WIDTH), k_map(0)),
                pl.BlockSpec((blk, KV_WIDTH), k_map(1)), pl.BlockSpec((dims.n_ctx, KV_WIDTH), ctx_map)]
    return pl.pallas_call(
        functools.partial(_attn_kernel, nb=nb, n_ctx=dims.n_ctx, j=j),
        grid=(dims.bsz, nb + nqc),
        in_specs=[pl.BlockSpec(memory_space=pltpu.SMEM), pl.BlockSpec((blk, ATTN_WIDTH), q_map)]
        + kv_specs + kv_specs,
        out_specs=pl.BlockSpec((blk, ATTN_WIDTH), q_map),
        out_shape=jax.ShapeDtypeStruct((rows_out, ATTN_WIDTH), BF16),
        compiler_params=_cparams("parallel", "parallel"),
        name="band_attention",
    )(sink, q, k, k, k, k, v, v, v, v)


def _halo_specs(tm, width, total_rows):
    per = tm // HALO_ROWS
    last = total_rows // HALO_ROWS - 1
    prev = pl.BlockSpec((HALO_ROWS, width), lambda i, *_: (jnp.maximum(i * per - 1, 0), 0))
    nxt = pl.BlockSpec((HALO_ROWS, width), lambda i, *_: (jnp.minimum((i + 1) * per, last), 0))
    return prev, nxt


def _outproj_even_kernel(*refs, dims, alpha, n_streams):
    x_refs, refs = refs[:n_streams], refs[n_streams:]
    a_ref, z_ref, zp_ref, zn_ref, gb_ref, cw_ref, w_ref, mod_ref, g_ref, b_ref, o_ref, cat_ref = refs
    tm = o_ref.shape[0]
    row = lax.broadcasted_iota(I32, (tm, 1), 0)
    pos, length = _seq_position(pl.program_id(0) * tm + row, dims)
    z = z_ref[...].astype(F32)
    zprev = jnp.where(row == 0, zp_ref[HALO_ROWS - 1:HALO_ROWS, :].astype(F32), pltpu.roll(z, 1, 0))
    zprev = jnp.where(pos == 0, 0.0, zprev)
    znext = jnp.where(row == tm - 1, zn_ref[0:1, :].astype(F32), pltpu.roll(z, tm - 1, 0))
    znext = jnp.where(pos == length - 1, 0.0, znext)
    y = zprev * cw_ref[0:1, :] + z * cw_ref[1:2, :] + znext * cw_ref[2:3, :]
    cat_ref[:, 0:ATTN_WIDTH] = a_ref[...]
    cat_ref[:, ATTN_WIDTH:] = (gb_ref[...].astype(F32) * y).astype(BF16)
    mix = jnp.dot(cat_ref[...], w_ref[...], preferred_element_type=F32)
    o_ref[...] = _residual_ln(_stream_rows(x_refs, dims), mix, mod_ref[0, 2:3, :], g_ref[...], b_ref[...], alpha)


def _outproj_even(a, z, gb, conv_w, w_out, streams, mods, ln_g, ln_b, dims, rows, alpha, layer, j):
    d = dims.d
    conv_width = z.shape[1]
    tm = _pick_tile(512, dims.n, dims.r_ctx)
    zp_spec, zn_spec = _halo_specs(tm, conv_width, z.shape[0])
    return pl.pallas_call(
        functools.partial(_outproj_even_kernel, dims=dims, alpha=alpha, n_streams=len(streams)),
        grid=(rows // tm,),
        in_specs=_stream_specs(streams, tm, dims)
        + [_row_spec(tm, ATTN_WIDTH), _row_spec(tm, conv_width), zp_spec, zn_spec,
           _row_spec(tm, conv_width), _layer_spec(conv_w, j), _layer_spec(w_out, layer),
           _mod_spec(dims, tm, layer), _layer_spec(ln_g, 2 * layer), _layer_spec(ln_b, 2 * layer)],
        out_specs=_row_spec(tm, d),
        out_shape=jax.ShapeDtypeStruct((rows, d), F32),
        scratch_shapes=[pltpu.VMEM((tm, d), BF16)],
        compiler_params=_cparams("parallel"),
        name="outproj_even",
    )(*streams, a, z, z, z, gb, conv_w, w_out, mods, ln_g, ln_b)


def _ffn_kernel(*refs, alpha, n_cast):
    casts_in, refs = refs[:n_cast], refs[n_cast:]
    x_ref, mod_ref, wg_ref, wu_ref, wd_ref, g_ref, b_ref, o_ref = refs[:8]
    casts_out, (h_ref, acc_ref) = refs[8:8 + n_cast], refs[8 + n_cast:]
    for src_ref, dst_ref in zip(casts_in, casts_out):
        dst_ref[...] = src_ref[...].astype(BF16)
    k = pl.program_id(1)

    @pl.when(k == 0)
    def _():
        h_ref[...] = (x_ref[...] * (1.0 + mod_ref[0, 4:5, :]) + mod_ref[0, 3:4, :]).astype(BF16)
        acc_ref[...] = jnp.zeros_like(acc_ref)

    h = h_ref[...]
    gate = jnp.dot(h, wg_ref[...], preferred_element_type=F32)
    up = jnp.dot(h, wu_ref[...], preferred_element_type=F32)
    act = (_silu(gate) * up).astype(BF16)
    acc_ref[...] += jnp.dot(act, wd_ref[...], preferred_element_type=F32)

    @pl.when(k == pl.num_programs(1) - 1)
    def _():
        o_ref[...] = _residual_ln(x_ref[...], acc_ref[...], mod_ref[0, 5:6, :], g_ref[...], b_ref[...], alpha)


def _chunk_major(w, tf):
    layers, d, dff = w.shape
    return w.astype(BF16).reshape(layers, d, dff // tf, tf).transpose(0, 2, 1, 3)


def _ffn(xs, mods, wg, wu, wd, ln_g, ln_b, dims, rows, alpha, layer, j, experts=None):
    _, nk, d, tf = wg.shape
    tm = _pick_tile(512, dims.n, dims.r_ctx)
    n_row = rows // tm
    in_specs = [pl.BlockSpec((tm, d), lambda i, k: (i, 0)), _mod_spec(dims, tm, layer),
                pl.BlockSpec((None, None, d, tf), lambda i, k: (j, k, 0, 0)),
                pl.BlockSpec((None, None, d, tf), lambda i, k: (j, k, 0, 0)),
                pl.BlockSpec((None, tf, d), lambda i, k: (j, k, 0)),
                _layer_spec(ln_g, 2 * layer + 1), _layer_spec(ln_b, 2 * layer + 1)]
    out_specs = [pl.BlockSpec((tm, d), lambda i, k: (i, 0))]
    out_shape = [jax.ShapeDtypeStruct((rows, d), F32)]
    cast_in, cast_specs = [], []
    if experts is not None:
        (eg, eu, ed), jj = experts
        _, n_exp, _, dff = eg.shape
        assert dff == nk * tf
        rb = 16
        while n_row * rb < n_exp * d:
            rb *= 2
        nrb = n_exp * d // rb
        rbd = n_exp * tf // nrb
        assert n_exp * d % rb == 0 and n_exp * tf % nrb == 0 and rbd % 16 == 0
        gu_in = pl.BlockSpec((None, rb, tf), lambda i, k: (jj, jnp.minimum(i, nrb - 1), k))
        dn_in = pl.BlockSpec((None, rbd, d), lambda i, k: (jj, jnp.minimum(i, nrb - 1) * nk + k, 0))
        cast_in = [eg.reshape(-1, n_exp * d, dff), eu.reshape(-1, n_exp * d, dff), ed.reshape(-1, n_exp * dff, d)]
        cast_specs = [gu_in, gu_in, dn_in]
        gu_out = pl.BlockSpec((None, rb, tf), lambda i, k: (k, i, 0))
        out_specs += [gu_out, gu_out, pl.BlockSpec((rbd, d), lambda i, k: (i * nk + k, 0))]
        out_shape += [jax.ShapeDtypeStruct((nk, n_row * rb, tf), BF16)] * 2
        out_shape += [jax.ShapeDtypeStruct((n_row * nk * rbd, d), BF16)]
    res = pl.pallas_call(
        functools.partial(_ffn_kernel, alpha=alpha, n_cast=len(cast_in)),
        grid=(n_row, nk),
        in_specs=cast_specs + in_specs,
        out_specs=out_specs,
        out_shape=out_shape,
        scratch_shapes=[pltpu.VMEM((tm, d), BF16), pltpu.VMEM((tm, d), F32)],
        compiler_params=_cparams("arbitrary", "arbitrary"),
        name="dense_ffn",
    )(*cast_in, xs, mods, wg, wu, wd, ln_g, ln_b)
    return res[0], (tuple(res[1:]) if experts is not None else None)


def _inproj_odd_kernel(x_ref, mod_ref, w_ref, up_ref, uf_ref):
    h = (x_ref[...] * (1.0 + mod_ref[0, 1:2, :]) + mod_ref[0, 0:1, :]).astype(BF16)
    pw = up_ref.shape[1]
    up_ref[...] = jnp.dot(h, w_ref[:, 0:pw], preferred_element_type=F32).astype(BF16)
    uf_ref[...] = jnp.dot(h, w_ref[:, pw:], preferred_element_type=F32).astype(BF16)


def _inproj_odd(xs, mods, w_in, dims, rows, pool_width, layer, j):
    d = dims.d
    four_width = w_in.shape[2] - pool_width
    tm = _pick_tile(512, dims.n, dims.r_ctx)
    return pl.pallas_call(
        _inproj_odd_kernel,
        grid=(rows // tm,),
        in_specs=[_row_spec(tm, d), _mod_spec(dims, tm, layer), _layer_spec(w_in, j)],
        out_specs=[_row_spec(tm, pool_width), _row_spec(tm, four_width)],
        out_shape=[jax.ShapeDtypeStruct((rows, pool_width), BF16), jax.ShapeDtypeStruct((rows, four_width), BF16)],
        compiler_params=_cparams("parallel"),
        name="inproj_odd",
    )(xs, mods, w_in)


def _seq_dft_constants(n, group):
    n2 = n // FFT_RADIX
    b = np.arange(n2, dtype=np.float64)
    ka = np.arange(FFT_RADIX, dtype=np.float64)
    ang = 2.0 * np.pi * np.outer(b, ka) / n
    tw = np.concatenate([np.cos(ang), -np.sin(ang)], axis=1)
    ang2 = 2.0 * np.pi * np.outer(b, b) / n2
    cs, sn = np.cos(ang2), np.sin(ang2)
    mat = np.block([[cs, sn], [-sn, cs]]) / np.sqrt(float(n) * group)
    return jnp.asarray(tw, F32), jnp.asarray(mat, BF16)


def _seq_dft_kernel(x_ref, tw_ref, m_ref, or_ref, oi_ref):
    n2 = x_ref.shape[1]
    lanes = x_ref.shape[2]
    for ka in range(FFT_RADIX):
        yr = yi = None
        for a in range(FFT_RADIX):
            ang = 2.0 * np.pi * ((a * ka) % FFT_RADIX) / FFT_RADIX
            cr, ci = float(np.round(np.cos(ang), 12)), float(np.round(-np.sin(ang), 12))
            xa = x_ref[a].astype(F32)
            if cr != 0.0:
                yr = cr * xa if yr is None else yr + cr * xa
            if ci != 0.0:
                yi = ci * xa if yi is None else yi + ci * xa
        if ka == 0:
            zr, zi = yr, jnp.zeros_like(yr)
        else:
            twr = jnp.broadcast_to(tw_ref[:, ka:ka + 1], (n2, lanes))
            twi = jnp.broadcast_to(tw_ref[:, FFT_RADIX + ka:FFT_RADIX + ka + 1], (n2, lanes))
            if yi is None:
                zr, zi = yr * twr, yr * twi
            else:
                zr, zi = yr * twr - yi * twi, yr * twi + yi * twr
        zcat = jnp.concatenate([zr, zi], axis=0).astype(BF16)
        res = jnp.dot(m_ref[...], zcat, preferred_element_type=F32)
        or_ref[ka] = res[0:n2].astype(BF16)
        oi_ref[ka] = res[n2:].astype(BF16)


def _seq_dft(uf, row0, bsz, n, group):
    rows, width = uf.shape
    n2 = n // FFT_RADIX
    lanes = _pick_tile(FFT_LANES, width)
    tw, mat = _seq_dft_constants(n, group)
    x3 = uf.reshape(rows // n2, n2, width)
    slab0 = row0 // n
    out = jax.ShapeDtypeStruct((bsz, FFT_RADIX, n2, width), BF16)
    ospec = pl.BlockSpec((None, FFT_RADIX, n2, lanes), lambda b, c: (b, 0, 0, c))
    return pl.pallas_call(
        _seq_dft_kernel,
        grid=(bsz, width // lanes),
        in_specs=[pl.BlockSpec((FFT_RADIX, n2, lanes), lambda b, c: (slab0 + b, 0, c)),
                  pl.BlockSpec(tw.shape, lambda b, c: (0, 0)),
                  pl.BlockSpec(mat.shape, lambda b, c: (0, 0), pipeline_mode=pl.Buffered(1))],
        out_specs=[ospec, ospec],
        out_shape=[out, out],
        compiler_params=_cparams("parallel", "parallel"),
        name="seq_dft",
    )(x3, tw, mat)


def _top2_routing(logits, n_exp):
    lane = lax.broadcasted_iota(I32, logits.shape, 1)
    lg = jnp.where(lane < n_exp, logits, -jnp.inf)
    m1 = jnp.max(lg, axis=-1, keepdims=True)
    i1 = jnp.min(jnp.where(lg == m1, lane, ROUTE_LANES), axis=-1, keepdims=True)
    lg2 = jnp.where(lane == i1, -jnp.inf, lg)
    m2 = jnp.max(lg2, axis=-1, keepdims=True)
    i2 = jnp.min(jnp.where(lg2 == m2, lane, ROUTE_LANES), axis=-1, keepdims=True)
    e2 = jnp.exp(m2 - m1)
    g1 = 1.0 / (1.0 + e2)
    g2 = e2 * g1
    return jnp.where(lane == 0, i1.astype(F32),
                     jnp.where(lane == 1, i2.astype(F32), jnp.where(lane == 2, g1, jnp.where(lane == 3, g2, 0.0))))


def _outproj_odd_kernel(*refs, dims, alpha, has_ctx, n_exp):
    (up_ref, upp_ref, upn_ref, xr_ref, xi_ref) = refs[:5]
    refs = refs[5:]
    if has_ctx:
        (cr_ref, ci_ref) = refs[:2]
        refs = refs[2:]
    (perm_ref, cs_ref, fw_ref, pw_ref, ps_ref, w_ref, x_ref, mod_ref, g_ref, b_ref, rwh_ref, rwl_ref,
     o_ref, h2_ref, route_ref, ext_ref, cat_ref) = refs
    tm = x_ref.shape[0]
    i = pl.program_id(0)
    row = lax.broadcasted_iota(I32, (tm, 1), 0)
    pos, length = _seq_position(i * tm + row, dims)
    first_pos, _ = _seq_position(i * tm, dims)
    last_pos, last_len = _seq_position(i * tm + tm - 1, dims)

    pool_width = up_ref.shape[1]
    group = pool_width // len(POOL_WINDOWS)
    h0 = POOL_HALO
    ext_ref[0:h0, :] = jnp.where(first_pos == 0, 0.0, upp_ref[HALO_ROWS - h0:HALO_ROWS, :].astype(F32))
    ext_ref[h0:h0 + tm, :] = up_ref[...].astype(F32)
    ext_ref[h0 + tm:h0 + tm + h0, :] = jnp.where(last_pos == last_len - 1, 0.0, upn_ref[0:h0, :].astype(F32))
    for gi, w in enumerate(POOL_WINDOWS):
        cols = slice(gi * group, (gi + 1) * group)
        back, ahead = w // 2, w - w // 2 - 1
        tot = None
        for s in range(-back, ahead + 1):
            t = ext_ref[h0 + s:h0 + s + tm, cols]
            tot = t if tot is None else tot + t
        cnt = jnp.minimum(pos + ahead, length - 1) - jnp.maximum(pos - back, 0) + 1
        pooled = tot / cnt.astype(F32) - ext_ref[h0:h0 + tm, cols]
        yp = jnp.dot(pooled.astype(BF16), pw_ref[gi], preferred_element_type=F32) * ps_ref[:, cols]
        cat_ref[:, cols] = yp.astype(BF16)

    if has_ctx:
        lat = i * tm < dims.r_lat
        xr = jnp.where(lat, xr_ref[...], cr_ref[...])
        xi = jnp.where(lat, xi_ref[...], ci_ref[...])
    else:
        xr, xi = xr_ref[...], xi_ref[...]
    four_width = xr.shape[-1]
    fgroup = four_width // FOURIER_HEADS
    xr = jnp.dot(perm_ref[...], xr.reshape(tm, four_width), preferred_element_type=F32).astype(BF16)
    xi = jnp.dot(perm_ref[...], xi.reshape(tm, four_width), preferred_element_type=F32).astype(BF16)
    for hd in range(FOURIER_HEADS):
        cols = slice(hd * fgroup, (hd + 1) * fgroup)
        both = jnp.concatenate([xr[:, cols], xi[:, cols]], axis=1)
        f = jnp.dot(both, cs_ref[...], preferred_element_type=F32)
        yf = jnp.dot(f.astype(BF16), fw_ref[hd], preferred_element_type=F32)
        cat_ref[:, pool_width + hd * fgroup:pool_width + (hd + 1) * fgroup] = yf.astype(BF16)

    mix = jnp.dot(cat_ref[...], w_ref[...], preferred_element_type=F32)
    xn = _residual_ln(x_ref[...], mix, mod_ref[0, 2:3, :], g_ref[...], b_ref[...], alpha)
    o_ref[...] = xn
    h2 = xn * (1.0 + mod_ref[0, 4:5, :]) + mod_ref[0, 3:4, :]
    h2_ref[...] = h2
    h_hi = h2.astype(BF16)
    h_lo = (h2 - h_hi.astype(F32)).astype(BF16)
    logits = (jnp.dot(h_hi, rwh_ref[...], preferred_element_type=F32)
              + (jnp.dot(h_lo, rwh_ref[...], preferred_element_type=F32)
                 + jnp.dot(h_hi, rwl_ref[...], preferred_element_type=F32)))
    route_ref[...] = _top2_routing(logits, n_exp)


def _channel_dft_matrix(group):
    c = np.arange(group, dtype=np.float64)
    ang = 2.0 * np.pi * np.outer(c, c) / group
    return jnp.asarray(np.concatenate([np.cos(ang), np.sin(ang)], axis=0), BF16)


def _tile_permutation(tm):
    per = tm // FFT_RADIX
    p = np.zeros((tm, tm), np.float32)
    for ka in range(FFT_RADIX):
        for kb in range(per):
            p[kb * FFT_RADIX + ka, ka * per + kb] = 1.0
    return jnp.asarray(p, BF16)


def _outproj_odd(up, lat_ri, ctx_ri, fourier_w, pool_w, pool_scale, w_out, xs, mods, ln_g, ln_b, rw_hi, rw_lo,
                 n_exp, dims, rows, alpha, layer, j):
    d = dims.d
    tm = SEQ_TILE
    pool_width = up.shape[1]
    four_width = lat_ri[0].shape[-1]
    fgroup = four_width // FOURIER_HEADS
    has_ctx = ctx_ri is not None
    per = tm // FFT_RADIX
    lat_tiles = dims.n // tm
    n_lat = dims.r_lat // tm
    upp_spec, upn_spec = _halo_specs(tm, pool_width, up.shape[0])

    def lat_map(i):
        t = jnp.minimum(i, n_lat - 1)
        return (t // lat_tiles, 0, t % lat_tiles, 0)

    lat_spec = pl.BlockSpec((None, FFT_RADIX, per, four_width), lat_map)
    operands = [up, up, up, lat_ri[0], lat_ri[1]]
    specs = [_row_spec(tm, pool_width), upp_spec, upn_spec, lat_spec, lat_spec]
    if has_ctx:
        ctx_tiles = dims.n_ctx // tm

        def ctx_map(i):
            t = jnp.maximum(i - n_lat, 0)
            return (t // ctx_tiles, 0, t % ctx_tiles, 0)

        ctx_spec = pl.BlockSpec((None, FFT_RADIX, per, four_width), ctx_map)
        operands += [ctx_ri[0], ctx_ri[1]]
        specs += [ctx_spec, ctx_spec]
    perm, cs = _tile_permutation(tm), _channel_dft_matrix(fgroup)
    operands += [perm, cs, fourier_w, pool_w, pool_scale, w_out, xs, mods, ln_g, ln_b, rw_hi, rw_lo]
    specs += [_const_spec(perm.shape), _const_spec(cs.shape), _layer_spec(fourier_w, j), _layer_spec(pool_w, j),
              _layer_spec(pool_scale, j), _layer_spec(w_out, layer), _row_spec(tm, d),
              _mod_spec(dims, tm, layer), _layer_spec(ln_g, 2 * layer), _layer_spec(ln_b, 2 * layer),
              _layer_spec(rw_hi, j), _layer_spec(rw_lo, j)]
    return pl.pallas_call(
        functools.partial(_outproj_odd_kernel, dims=dims, alpha=alpha, has_ctx=has_ctx, n_exp=n_exp),
        grid=(rows // tm,),
        in_specs=specs,
        out_specs=[_row_spec(tm, d), _row_spec(tm, d), _row_spec(tm, ROUTE_LANES)],
        out_shape=[jax.ShapeDtypeStruct((rows, d), F32), jax.ShapeDtypeStruct((rows, d), F32),
                   jax.ShapeDtypeStruct((rows, ROUTE_LANES), F32)],
        scratch_shapes=[pltpu.VMEM((tm + 2 * POOL_HALO, pool_width), F32), pltpu.VMEM((tm, d), BF16)],
        compiler_params=_cparams("parallel"),
        name="outproj_odd",
    )(*operands)


def _dispatch_plan(route, n_exp, tile):
    tokens = route.shape[0]
    assign = TOP_K * tokens
    experts = route[:, 0:TOP_K].astype(I32).reshape(assign)
    onehot = (experts[:, None] == jnp.arange(n_exp, dtype=I32)[None, :]).astype(I32)
    running = jnp.cumsum(onehot, axis=0)
    rank = jnp.sum((running - onehot) * onehot, axis=1)
    counts = running[-1]
    padded = ((counts + tile - 1) // tile) * tile
    ends = jnp.cumsum(padded)
    starts = ends - padded
    slot = jnp.sum(onehot * starts[None, :], axis=1) + rank
    n_tiles = -(-(assign + n_exp * (tile - 1)) // tile)
    tile_start = jnp.arange(n_tiles, dtype=I32) * tile
    tile_expert = jnp.minimum(jnp.sum((tile_start[:, None] >= ends[None, :]).astype(I32), axis=1), n_exp - 1)
    live_tiles = ends[-1] // tile
    tile_expert = jnp.where(tile_start < ends[-1], tile_expert, tile_expert[jnp.maximum(live_tiles - 1, 0)])
    pad_first = starts + counts
    pad_head = jnp.minimum((-pad_first) % SUBLANES, padded - counts)
    pad_info = jnp.concatenate([pad_first, pad_head, pad_first + pad_head, padded - counts - pad_head,
                                live_tiles[None]]).astype(I32)
    return slot, pad_info, tile_expert, n_tiles


def _dispatch_kernel(slot_ref, pad_ref, h_ref, o_ref, zero_ref, sem, zsem, *, n_exp, tile, n_tiles, min_tiles):
    i = pl.program_id(0)
    tm = h_ref.shape[0]

    def zero_copy(first_row, size):
        return pltpu.make_async_copy(zero_ref.at[pl.ds(0, size)], o_ref.at[pl.ds(first_row, size)], zsem)

    def padding_copies(act):
        for e in range(n_exp):
            first, head = pad_ref[e], pad_ref[n_exp + e]
            for r in range(SUBLANES - 1):
                pl.when(r < head)(functools.partial(act, zero_copy(first + r, 1)))
            first, length = pad_ref[2 * n_exp + e], pad_ref[3 * n_exp + e]
            for bit in reversed(range(SUBLANES.bit_length() - 1, tile.bit_length() - 1)):
                size = 1 << bit
                for part in range(-(-size // ZERO_ROWS)):
                    rows = min(size, ZERO_ROWS)
                    at = pl.multiple_of(first + part * rows, SUBLANES)
                    pl.when((length & size) != 0)(functools.partial(act, zero_copy(at, rows)))
                first = first + (length & size)
        live = pad_ref[PAD_LIVE * n_exp]
        for t in range(min_tiles, n_tiles):
            for part in range(tile // ZERO_ROWS):
                pl.when(t >= live)(functools.partial(act, zero_copy(t * tile + part * ZERO_ROWS, ZERO_ROWS)))

    @pl.when(i == 0)
    def _():
        zero_ref[...] = jnp.zeros_like(zero_ref)
        padding_copies(lambda cp: cp.start())
        padding_copies(lambda cp: cp.wait())

    base = i * tm * TOP_K

    def issue(r, carry):
        for c in range(TOP_K):
            dst = slot_ref[base + r * TOP_K + c]
            pltpu.make_async_copy(h_ref.at[pl.ds(r, 1)], o_ref.at[pl.ds(dst, 1)], sem).start()
        return carry

    lax.fori_loop(0, tm, issue, 0)
    for c in range(TOP_K):
        pltpu.make_async_copy(h_ref, o_ref.at[pl.ds(0, tm)], sem).wait()


def _dispatch_rows(slot, pad_info, h2, n_exp, tile, n_tiles):
    tokens, d = h2.shape
    tm = SEQ_TILE
    min_tiles = -(-(TOP_K * tokens) // tile)
    return pl.pallas_call(
        functools.partial(_dispatch_kernel, n_exp=n_exp, tile=tile, n_tiles=n_tiles, min_tiles=min_tiles),
        grid_spec=pltpu.PrefetchScalarGridSpec(
            num_scalar_prefetch=2,
            grid=(tokens // tm,),
            in_specs=[_row_spec(tm, d)],
            out_specs=pl.BlockSpec(memory_space=pl.ANY),
            scratch_shapes=[pltpu.VMEM((ZERO_ROWS, d), h2.dtype), pltpu.SemaphoreType.DMA(()),
                            pltpu.SemaphoreType.DMA(())]),
        out_shape=jax.ShapeDtypeStruct((n_tiles * tile, d), h2.dtype),
        compiler_params=_cparams("arbitrary", unchecked=True),
        name="moe_dispatch",
    )(slot, pad_info, h2)


def _expert_kernel(te_ref, pad_ref, x_ref, wg_ref, wu_ref, wd_ref, o_ref, h_ref, acc_ref, *, n_exp):
    i, k = pl.program_id(0), pl.program_id(1)
    last = pl.num_programs(1) - 1
    live = i < pad_ref[PAD_LIVE * n_exp]

    @pl.when(jnp.logical_and(live, k == 0))
    def _():
        h_ref[...] = x_ref[...].astype(BF16)
        acc_ref[...] = jnp.zeros_like(acc_ref)

    @pl.when(live)
    def _():
        h = h_ref[...]
        gate = jnp.dot(h, wg_ref[...], preferred_element_type=F32)
        up = jnp.dot(h, wu_ref[...], preferred_element_type=F32)
        act = (_silu(gate) * up).astype(BF16)
        acc_ref[...] += jnp.dot(act, wd_ref[...], preferred_element_type=F32)

    @pl.when(jnp.logical_and(live, k == last))
    def _():
        o_ref[...] = acc_ref[...]

    @pl.when(jnp.logical_and(jnp.logical_not(live), k == last))
    def _():
        o_ref[...] = jnp.zeros_like(o_ref)


def _expert_ffn(tile_expert, pad_info, xs, wg, wu, wd, n_exp):
    n_slots, d = xs.shape
    nk, _, tf = wg.shape
    tm = EXPERT_TILE

    def chunk(i, k, pad):
        return jnp.where(i < pad[PAD_LIVE * n_exp], k, nk - 1)

    return pl.pallas_call(
        functools.partial(_expert_kernel, n_exp=n_exp),
        grid_spec=pltpu.PrefetchScalarGridSpec(
            num_scalar_prefetch=2,
            grid=(n_slots // tm, nk),
            in_specs=[pl.BlockSpec((tm, d), lambda i, k, te, pad: (i, 0)),
                      pl.BlockSpec((None, d, tf), lambda i, k, te, pad: (chunk(i, k, pad), te[i], 0)),
                      pl.BlockSpec((None, d, tf), lambda i, k, te, pad: (chunk(i, k, pad), te[i], 0)),
                      pl.BlockSpec((tf, d), lambda i, k, te, pad: (te[i] * nk + chunk(i, k, pad), 0))],
            out_specs=pl.BlockSpec((tm, d), lambda i, k, te, pad: (i, 0)),
            scratch_shapes=[pltpu.VMEM((tm, d), BF16), pltpu.VMEM((tm, d), F32)]),
        out_shape=jax.ShapeDtypeStruct((n_slots, d), F32),
        compiler_params=_cparams("arbitrary", "arbitrary"),
        name="expert_ffn",
    )(tile_expert, pad_info, xs, wg, wu, wd)


def _combine_kernel(slot_ref, y_ref, route_ref, x_ref, mod_ref, g_ref, b_ref, o_ref, buf_ref, sem, *, alpha):
    tm = x_ref.shape[0]
    i = pl.program_id(0)

    def fetch(tile, half):
        base = tile * tm * TOP_K

        def issue(r, carry):
            for c in range(TOP_K):
                src = slot_ref[base + r * TOP_K + c]
                pltpu.make_async_copy(y_ref.at[pl.ds(src, 1)], buf_ref.at[half, c, pl.ds(r, 1)],
                                      sem.at[half]).start()
            return carry

        lax.fori_loop(0, tm, issue, 0)

    @pl.when(i == 0)
    def _():
        fetch(0, 0)

    @pl.when(i + 1 < pl.num_programs(0))
    def _():
        fetch(i + 1, (i + 1) % 2)

    half = i % 2
    for c in range(TOP_K):
        pltpu.make_async_copy(y_ref.at[pl.ds(0, tm)], buf_ref.at[half, c], sem.at[half]).wait()
    route = route_ref[...]
    ff = route[:, 2:3] * buf_ref[half, 0] + route[:, 3:4] * buf_ref[half, 1]
    o_ref[...] = _residual_ln(x_ref[...], ff, mod_ref[0, 5:6, :], g_ref[...], b_ref[...], alpha)


def _combine(slot, ys, route, xs, mods, ln_g, ln_b, dims, rows, alpha, layer):
    d = dims.d
    tm = SEQ_TILE
    return pl.pallas_call(
        functools.partial(_combine_kernel, alpha=alpha),
        grid_spec=pltpu.PrefetchScalarGridSpec(
            num_scalar_prefetch=1,
            grid=(rows // tm,),
            in_specs=[pl.BlockSpec(memory_space=pl.ANY), _row_spec(tm, ROUTE_LANES), _row_spec(tm, d),
                      _mod_spec(dims, tm, layer), _layer_spec(ln_g, 2 * layer + 1),
                      _layer_spec(ln_b, 2 * layer + 1)],
            out_specs=_row_spec(tm, d),
            scratch_shapes=[pltpu.VMEM((2, TOP_K, tm, d), F32), pltpu.SemaphoreType.DMA((2,))]),
        out_shape=jax.ShapeDtypeStruct((rows, d), F32),
        compiler_params=_cparams("arbitrary", unchecked=True),
        name="moe_combine",
    )(slot, ys, route, xs, mods, ln_g, ln_b)


def _moe(xs, h2, route, mods, wg, wu, wd, n_exp, ln_g, ln_b, dims, rows, alpha, layer):
    slot, pad_info, tile_expert, n_tiles = _dispatch_plan(route, n_exp, EXPERT_TILE)
    sorted_rows = _dispatch_rows(slot, pad_info, h2, n_exp, EXPERT_TILE, n_tiles)
    ys = _expert_ffn(tile_expert, pad_info, sorted_rows, wg, wu, wd, n_exp)
    return _combine(slot, ys, route, xs, mods, ln_g, ln_b, dims, rows, alpha, layer)


def kernel(x, c, ctx, c_ctx, w_mod, b_mod, w_mix_out, ln_g, ln_b, w_in_ab, conv_w, attn_sink, w_in_cd, pool_w,
           pool_scale, fourier_w, ffn_w_gate, ffn_w_up, ffn_w_down, router_w, moe_w_gate, moe_w_up, moe_w_down):
    bsz, n, d = x.shape
    n_ctx = ctx.shape[1]
    depth = w_mod.shape[0]
    n_exp = router_w.shape[2]
    dims = Dims(bsz, n, n_ctx, d)
    assert n % SEQ_TILE == 0 and n_ctx % SEQ_TILE == 0 and n % GRID_W == 0
    assert dims.r_lat % (FFT_RADIX * (n_ctx // FFT_RADIX)) == 0 and dims.r_all % (n // FFT_RADIX) == 0
    alpha = (2.0 * depth) ** 0.25
    pool_width = pool_w.shape[1] * pool_w.shape[2]
    fgroup = fourier_w.shape[2]

    mods = _modulations(c, c_ctx, w_mod, b_mod)
    w_out = w_mix_out.astype(BF16)
    w_ab, w_cd = w_in_ab.astype(BF16), w_in_cd.astype(BF16)
    tf = _pick_tile(512, ffn_w_gate.shape[2], moe_w_gate.shape[3])
    ffn_g, ffn_u, ffn_d = _chunk_major(ffn_w_gate, tf), _chunk_major(ffn_w_up, tf), ffn_w_down.astype(BF16)
    four_w, pool_wb = fourier_w.astype(BF16), pool_w.astype(BF16)
    pool_sc = pool_scale[:, None, :]
    lng, lnb = ln_g.reshape(2 * depth, 1, d), ln_b.reshape(2 * depth, 1, d)
    rw = jnp.zeros(router_w.shape[:2] + (ROUTE_LANES,), F32).at[:, :, :n_exp].set(router_w)
    rw_hi = rw.astype(BF16)
    rw_lo = (rw - rw_hi.astype(F32)).astype(BF16)

    streams = [x.reshape(dims.r_lat, d), ctx.reshape(dims.r_ctx, d)]
    expert_bf = None
    for l in range(depth):
        even = l % 2 == 0
        j = l // 2
        ctx_out = any(m % 2 == 0 for m in range(l + 1, depth))
        rows = dims.r_all if ctx_out else dims.r_lat
        if even:
            q, k, v, z, gb = _inproj_even(streams, mods, w_ab, dims, l, j)
            a = _attention(q, k, v, attn_sink, dims, ctx_out, j)
            xs = _outproj_even(a, z, gb, conv_w, w_out, streams, mods, lng, lnb, dims, rows, alpha, l, j)
            experts = ((moe_w_gate, moe_w_up, moe_w_down), j) if l + 1 < depth else None
            xs, expert_bf = _ffn(xs, mods, ffn_g, ffn_u, ffn_d, lng, lnb, dims, rows, alpha, l, j, experts)
        else:
            xs = streams[0]
            up, uf = _inproj_odd(xs, mods, w_cd, dims, rows, pool_width, l, j)
            lat_ri = _seq_dft(uf, 0, bsz, n, fgroup)
            ctx_ri = _seq_dft(uf, dims.r_lat, bsz, n_ctx, fgroup) if ctx_out else None
            xs, h2, route = _outproj_odd(up, lat_ri, ctx_ri, four_w, pool_wb, pool_sc, w_out, xs, mods, lng, lnb,
                                         rw_hi, rw_lo, n_exp, dims, rows, alpha, l, j)
            xs = _moe(xs, h2, route, mods, *expert_bf, n_exp, lng, lnb, dims, rows, alpha, l)
        streams = [xs]
    return xs[:dims.r_lat].reshape(bsz, n, d)
```

```python
import functools
from typing import NamedTuple

import numpy as np
import jax
import jax.numpy as jnp
from jax import lax
from jax.experimental import pallas as pl
from jax.experimental.pallas import tpu as pltpu

F32 = jnp.float32
BF16 = jnp.bfloat16
I32 = jnp.int32

HEAD_DIM = 128
N_Q_HEADS = 8
N_KV_HEADS = 2
Q_PER_KV = N_Q_HEADS // N_KV_HEADS
ATTN_WIDTH = N_Q_HEADS * HEAD_DIM
KV_WIDTH = N_KV_HEADS * HEAD_DIM
ATTN_BLOCK = 128
GRID_W = 64
ROPE_THETA = 10000.0
NEG_INF = -1e30
POOL_WINDOWS = (2, 4, 8, 16)
POOL_HALO = 8
HALO_ROWS = 16
FOURIER_HEADS = 4
N_MOD = 6
LN_EPS = 1e-5
TOP_K = 2
FFT_RADIX = 8
FFT_LANES = 256
SEQ_TILE = 256
ROUTE_LANES = 128
EXPERT_TILE = 512
ISSUE_UNROLL = 8
ZERO_ROWS = 256
SUBLANES = 8
PAD_LIVE = 4
V7X_VMEM_LIMIT = 56 * 1024 * 1024


class Dims(NamedTuple):
    bsz: int
    n: int
    n_ctx: int
    d: int

    @property
    def r_lat(self):
        return self.bsz * self.n

    @property
    def r_ctx(self):
        return self.bsz * self.n_ctx

    @property
    def r_all(self):
        return self.r_lat + self.r_ctx


def _pick_tile(pref, *dims):
    t = pref
    while any(d % t for d in dims):
        t //= 2
    return t


def _cparams(*sem, unchecked=False):
    return pltpu.CompilerParams(dimension_semantics=sem, vmem_limit_bytes=V7X_VMEM_LIMIT,
                                disable_bounds_checks=unchecked)


def _residual_ln(x, y, gate, g, b, alpha):
    v = alpha * x + gate * y
    mu = jnp.mean(v, axis=-1, keepdims=True)
    vc = v - mu
    var = jnp.mean(vc * vc, axis=-1, keepdims=True)
    return vc * lax.rsqrt(var + LN_EPS) * g + b


def _silu(t):
    return t * jax.nn.sigmoid(t)


def _seq_position(row, dims):
    lat = row < dims.r_lat
    pos = jnp.where(lat, lax.rem(row, dims.n), lax.rem(row - dims.r_lat, dims.n_ctx))
    length = jnp.where(lat, dims.n, dims.n_ctx)
    return pos, length


def _mod_spec(dims, tm, layer):
    return pl.BlockSpec((None, 1, N_MOD, dims.d), lambda i, *_: (layer, (i * tm) // dims.n, 0, 0))


def _row_spec(tm, width):
    return pl.BlockSpec((tm, width), lambda i, *_: (i, 0))


def _stream_specs(streams, tm, dims):
    if len(streams) == 1:
        return [_row_spec(tm, dims.d)]
    n_lat = dims.r_lat // tm
    return [pl.BlockSpec((tm, dims.d), lambda i, *_: (jnp.minimum(i, n_lat - 1), 0)),
            pl.BlockSpec((tm, dims.d), lambda i, *_: (jnp.maximum(i - n_lat, 0), 0))]


def _stream_rows(refs, dims):
    if len(refs) == 1:
        return refs[0][...]
    tm = refs[0].shape[0]
    return jnp.where(pl.program_id(0) * tm < dims.r_lat, refs[0][...], refs[1][...])


def _layer_spec(arr, index, **kw):
    zeros = (0,) * (arr.ndim - 1)
    return pl.BlockSpec((None,) + arr.shape[1:], lambda *_: (index,) + zeros, **kw)


def _const_spec(shape):
    zeros = (0,) * len(shape)
    return pl.BlockSpec(shape, lambda *_: zeros)


def _mod_kernel(cond_ref, w_ref, b_ref, o_ref):
    s = _silu(cond_ref[...]).astype(BF16)
    o_ref[...] = jnp.dot(s, w_ref[...].astype(BF16), preferred_element_type=F32) + b_ref[...]


def _modulations(c, c_ctx, w_mod, b_mod):
    depth, d, nmod = w_mod.shape
    bsz = c.shape[0]
    rows = -(-(bsz + 1) // 8) * 8
    cond = jnp.zeros((rows, d), F32).at[:bsz].set(c).at[bsz].set(c_ctx)
    tn = _pick_tile(1024, nmod)
    out = pl.pallas_call(
        _mod_kernel,
        grid=(depth, nmod // tn),
        in_specs=[pl.BlockSpec((rows, d), lambda l, j: (0, 0)),
                  pl.BlockSpec((None, d, tn), lambda l, j: (l, 0, j)),
                  pl.BlockSpec((None, 1, tn), lambda l, j: (l, 0, j))],
        out_specs=pl.BlockSpec((None, rows, tn), lambda l, j: (l, 0, j)),
        out_shape=jax.ShapeDtypeStruct((depth, rows, nmod), F32),
        compiler_params=_cparams("arbitrary", "arbitrary"),
        name="modulation",
    )(cond, w_mod, b_mod.reshape(depth, 1, nmod))
    return out.reshape(depth, rows, N_MOD, d)


def _rope_tables(n, tm):
    rows = n // GRID_W
    row = jnp.repeat(jnp.arange(rows, dtype=F32), GRID_W)
    col = jnp.tile(jnp.arange(GRID_W, dtype=F32), rows)
    half = HEAD_DIM // 2
    inv = ROPE_THETA ** (-jnp.arange(0, half, 2, dtype=F32) / half)
    ang_r = row[:, None] * inv
    ang_c = col[:, None] * inv
    ang = jnp.concatenate([ang_r, ang_r, ang_c, ang_c], -1)
    cos, sin = jnp.cos(ang), jnp.sin(ang)
    quarter = HEAD_DIM // 4
    first = (jnp.arange(HEAD_DIM) // quarter) % 2 == 0
    sin_up = jnp.where(first, -sin, 0.0)
    sin_dn = jnp.where(first, 0.0, sin)
    ident = jnp.zeros((tm, HEAD_DIM), F32)
    return (jnp.concatenate([cos, ident + 1.0], 0), jnp.concatenate([sin_up, ident], 0),
            jnp.concatenate([sin_dn, ident], 0))


def _inproj_even_kernel(*refs, conv_width, dims, n_streams):
    x_refs, refs = refs[:n_streams], refs[n_streams:]
    mod_ref, w_ref, cos_ref, sup_ref, sdn_ref, q_ref, k_ref, v_ref, z_ref, gb_ref = refs
    h = (_stream_rows(x_refs, dims) * (1.0 + mod_ref[0, 1:2, :]) + mod_ref[0, 0:1, :]).astype(BF16)
    cos, sup, sdn = cos_ref[...], sup_ref[...], sdn_ref[...]
    quarter = HEAD_DIM // 4

    def rope(u):
        return u * cos + pltpu.roll(u, HEAD_DIM - quarter, 1) * sup + pltpu.roll(u, quarter, 1) * sdn

    q = jnp.dot(h, w_ref[:, 0:ATTN_WIDTH], preferred_element_type=F32)
    for hd in range(N_Q_HEADS):
        sl = slice(hd * HEAD_DIM, (hd + 1) * HEAD_DIM)
        q_ref[:, sl] = rope(q[:, sl]).astype(BF16)
    o = ATTN_WIDTH
    kv = jnp.dot(h, w_ref[:, o:o + 2 * KV_WIDTH], preferred_element_type=F32)
    for hd in range(N_KV_HEADS):
        sl = slice(hd * HEAD_DIM, (hd + 1) * HEAD_DIM)
        k_ref[:, sl] = rope(kv[:, sl]).astype(BF16)
    v_ref[...] = kv[:, KV_WIDTH:].astype(BF16)
    o += 2 * KV_WIDTH
    u = jnp.dot(h, w_ref[:, o:o + conv_width], preferred_element_type=F32)
    gc = jnp.dot(h, w_ref[:, o + 2 * conv_width:o + 3 * conv_width], preferred_element_type=F32)
    z_ref[...] = (gc * u).astype(BF16)
    gb = jnp.dot(h, w_ref[:, o + conv_width:o + 2 * conv_width], preferred_element_type=F32)
    gb_ref[...] = gb.astype(BF16)


def _inproj_even(streams, mods, w_in, dims, layer, j):
    rows = dims.r_all
    conv_width = (w_in.shape[2] - ATTN_WIDTH - 2 * KV_WIDTH) // 3
    tm = _pick_tile(512, dims.n, dims.r_ctx)
    cos, sup, sdn = _rope_tables(dims.n, tm)
    nlat = dims.n // tm
    tab_spec = pl.BlockSpec((tm, HEAD_DIM), lambda i: (jnp.where(i * tm < dims.r_lat, i % nlat, nlat), 0))
    return pl.pallas_call(
        functools.partial(_inproj_even_kernel, conv_width=conv_width, dims=dims, n_streams=len(streams)),
        grid=(rows // tm,),
        in_specs=_stream_specs(streams, tm, dims)
        + [_mod_spec(dims, tm, layer), _layer_spec(w_in, j, pipeline_mode=pl.Buffered(1)),
           tab_spec, tab_spec, tab_spec],
        out_specs=[_row_spec(tm, ATTN_WIDTH), _row_spec(tm, KV_WIDTH), _row_spec(tm, KV_WIDTH),
                   _row_spec(tm, conv_width), _row_spec(tm, conv_width)],
        out_shape=[jax.ShapeDtypeStruct((rows, ATTN_WIDTH), BF16), jax.ShapeDtypeStruct((rows, KV_WIDTH), BF16),
                   jax.ShapeDtypeStruct((rows, KV_WIDTH), BF16), jax.ShapeDtypeStruct((rows, conv_width), BF16),
                   jax.ShapeDtypeStruct((rows, conv_width), BF16)],
        compiler_params=_cparams("parallel"),
        name="inproj_even",
    )(*streams, mods, w_in, cos, sup, sdn)


def _attn_kernel(sink_ref, q_ref, kp_ref, kc_ref, kn_ref, kx_ref, vp_ref, vc_ref, vn_ref, vx_ref, o_ref, *,
                 nb, n_ctx, j):
    i = pl.program_id(1)
    blk = ATTN_BLOCK
    rows = Q_PER_KV * blk
    latent = i < nb
    has_prev = jnp.logical_and(latent, i > 0)
    has_next = jnp.logical_and(latent, i < nb - 1)
    qi = lax.broadcasted_iota(I32, (blk, blk), 0)
    kj = lax.broadcasted_iota(I32, (blk, blk), 1)
    bias_prev = jnp.where(jnp.logical_and(kj >= qi, has_prev), 0.0, NEG_INF).astype(F32)
    bias_next = jnp.where(jnp.logical_and(kj <= qi, has_next), 0.0, NEG_INF).astype(F32)
    bias_prev = jnp.concatenate([bias_prev] * Q_PER_KV, axis=0)
    bias_next = jnp.concatenate([bias_next] * Q_PER_KV, axis=0)
    bias_own = jnp.where(latent, 0.0, NEG_INF).astype(F32)
    grp = lax.broadcasted_iota(I32, (rows, 1), 0) // blk
    scale = HEAD_DIM ** -0.5
    for hk in range(N_KV_HEADS):
        sl = slice(hk * HEAD_DIM, (hk + 1) * HEAD_DIM)
        kcat = jnp.concatenate([kx_ref[:, sl], kp_ref[:, sl], kc_ref[:, sl], kn_ref[:, sl]], axis=0)
        vcat = jnp.concatenate([vx_ref[:, sl], vp_ref[:, sl], vc_ref[:, sl], vn_ref[:, sl]], axis=0)
        heads = [hk * Q_PER_KV + g for g in range(Q_PER_KV)]
        qs = jnp.concatenate([q_ref[:, hq * HEAD_DIM:(hq + 1) * HEAD_DIM] for hq in heads], axis=0)
        s = lax.dot_general(qs, kcat, (((1,), (1,)), ((), ())), preferred_element_type=F32) * scale
        s = jnp.concatenate([s[:, 0:n_ctx], s[:, n_ctx:n_ctx + blk] + bias_prev,
                             s[:, n_ctx + blk:n_ctx + 2 * blk] + bias_own, s[:, n_ctx + 2 * blk:] + bias_next], axis=1)
        sink = jnp.full((rows, 1), sink_ref[j, heads[-1]], F32)
        for g in range(Q_PER_KV - 1):
            sink = jnp.where(grp == g, sink_ref[j, heads[g]], sink)
        m = jnp.maximum(jnp.max(s, axis=-1, keepdims=True), sink)
        p = jnp.exp(s - m)
        den = jnp.sum(p, axis=-1, keepdims=True) + jnp.exp(sink - m)
        pn = (p * (1.0 / den)).astype(BF16)
        o = jnp.dot(pn, vcat, preferred_element_type=F32)
        for g, hq in enumerate(heads):
            o_ref[:, hq * HEAD_DIM:(hq + 1) * HEAD_DIM] = o[g * blk:(g + 1) * blk].astype(BF16)


def _attention(q, k, v, sink, dims, ctx_out, j):
    blk = ATTN_BLOCK
    nb = dims.n // blk
    nqc = dims.n_ctx // blk if ctx_out else 0
    rows_out = dims.r_all if ctx_out else dims.r_lat
    lat_blocks = dims.r_lat // blk

    def q_map(b, i, *_):
        return (jnp.where(i < nb, b * nb + i, lat_blocks + b * nqc + (i - nb)), 0)

    def k_map(shift):
        return lambda b, i, *_: (b * nb + jnp.clip(i + shift, 0, nb - 1), 0)

    def ctx_map(b, i, *_):
        return (dims.r_lat // dims.n_ctx + b, 0)

    kv_specs = [pl.BlockSpec((blk, KV_WIDTH), k_map(-1)), pl.BlockSpec((blk, KV_WIDTH), k_map(0)),
                pl.BlockSpec((blk, KV_WIDTH), k_map(1)), pl.BlockSpec((dims.n_ctx, KV_WIDTH), ctx_map)]
    return pl.pallas_call(
        functools.partial(_attn_kernel, nb=nb, n_ctx=dims.n_ctx, j=j),
        grid=(dims.bsz, nb + nqc),
        in_specs=[pl.BlockSpec(memory_space=pltpu.SMEM), pl.BlockSpec((blk, ATTN_WIDTH), q_map)]
        + kv_specs + kv_specs,
        out_specs=pl.BlockSpec((blk, ATTN_WIDTH), q_map),
        out_shape=jax.ShapeDtypeStruct((rows_out, ATTN_WIDTH), BF16),
        compiler_params=_cparams("parallel", "parallel"),
        name="band_attention",
    )(sink, q, k, k, k, k, v, v, v, v)


def _halo_specs(tm, width, total_rows):
    per = tm // HALO_ROWS
    last = total_rows // HALO_ROWS - 1
    prev = pl.BlockSpec((HALO_ROWS, width), lambda i, *_: (jnp.maximum(i * per - 1, 0), 0))
    nxt = pl.BlockSpec((HALO_ROWS, width), lambda i, *_: (jnp.minimum((i + 1) * per, last), 0))
    return prev, nxt


def _outproj_even_kernel(*refs, dims, alpha, n_streams):
    x_refs, refs = refs[:n_streams], refs[n_streams:]
    a_ref, z_ref, zp_ref, zn_ref, gb_ref, cw_ref, w_ref, mod_ref, g_ref, b_ref, o_ref, cat_ref = refs
    tm = o_ref.shape[0]
    row = lax.broadcasted_iota(I32, (tm, 1), 0)
    pos, length = _seq_position(pl.program_id(0) * tm + row, dims)
    z = z_ref[...].astype(F32)
    zprev = jnp.where(row == 0, zp_ref[HALO_ROWS - 1:HALO_ROWS, :].astype(F32), pltpu.roll(z, 1, 0))
    zprev = jnp.where(pos == 0, 0.0, zprev)
    znext = jnp.where(row == tm - 1, zn_ref[0:1, :].astype(F32), pltpu.roll(z, tm - 1, 0))
    znext = jnp.where(pos == length - 1, 0.0, znext)
    y = zprev * cw_ref[0:1, :] + z * cw_ref[1:2, :] + znext * cw_ref[2:3, :]
    cat_ref[:, 0:ATTN_WIDTH] = a_ref[...]
    cat_ref[:, ATTN_WIDTH:] = (gb_ref[...].astype(F32) * y).astype(BF16)
    mix = jnp.dot(cat_ref[...], w_ref[...], preferred_element_type=F32)
    o_ref[...] = _residual_ln(_stream_rows(x_refs, dims), mix, mod_ref[0, 2:3, :], g_ref[...], b_ref[...], alpha)


def _outproj_even(a, z, gb, conv_w, w_out, streams, mods, ln_g, ln_b, dims, rows, alpha, layer, j):
    d = dims.d
    conv_width = z.shape[1]
    tm = _pick_tile(512, dims.n, dims.r_ctx)
    zp_spec, zn_spec = _halo_specs(tm, conv_width, z.shape[0])
    return pl.pallas_call(
        functools.partial(_outproj_even_kernel, dims=dims, alpha=alpha, n_streams=len(streams)),
        grid=(rows // tm,),
        in_specs=_stream_specs(streams, tm, dims)
        + [_row_spec(tm, ATTN_WIDTH), _row_spec(tm, conv_width), zp_spec, zn_spec,
           _row_spec(tm, conv_width), _layer_spec(conv_w, j), _layer_spec(w_out, layer),
           _mod_spec(dims, tm, layer), _layer_spec(ln_g, 2 * layer), _layer_spec(ln_b, 2 * layer)],
        out_specs=_row_spec(tm, d),
        out_shape=jax.ShapeDtypeStruct((rows, d), F32),
        scratch_shapes=[pltpu.VMEM((tm, d), BF16)],
        compiler_params=_cparams("parallel"),
        name="outproj_even",
    )(*streams, a, z, z, z, gb, conv_w, w_out, mods, ln_g, ln_b)


def _ffn_kernel(*refs, alpha, n_cast):
    casts_in, refs = refs[:n_cast], refs[n_cast:]
    x_ref, mod_ref, wg_ref, wu_ref, wd_ref, g_ref, b_ref, o_ref = refs[:8]
    casts_out, (h_ref, acc_ref) = refs[8:8 + n_cast], refs[8 + n_cast:]
    for src_ref, dst_ref in zip(casts_in, casts_out):
        dst_ref[...] = src_ref[...].astype(BF16)
    k = pl.program_id(1)

    @pl.when(k == 0)
    def _():
        h_ref[...] = (x_ref[...] * (1.0 + mod_ref[0, 4:5, :]) + mod_ref[0, 3:4, :]).astype(BF16)
        acc_ref[...] = jnp.zeros_like(acc_ref)

    h = h_ref[...]
    gate = jnp.dot(h, wg_ref[...], preferred_element_type=F32)
    up = jnp.dot(h, wu_ref[...], preferred_element_type=F32)
    act = (_silu(gate) * up).astype(BF16)
    acc_ref[...] += jnp.dot(act, wd_ref[...], preferred_element_type=F32)

    @pl.when(k == pl.num_programs(1) - 1)
    def _():
        o_ref[...] = _residual_ln(x_ref[...], acc_ref[...], mod_ref[0, 5:6, :], g_ref[...], b_ref[...], alpha)


def _chunk_major(w, tf):
    layers, d, dff = w.shape
    return w.astype(BF16).reshape(layers, d, dff // tf, tf).transpose(0, 2, 1, 3)


def _ffn(xs, mods, wg, wu, wd, ln_g, ln_b, dims, rows, alpha, layer, j, experts=None):
    _, nk, d, tf = wg.shape
    tm = _pick_tile(512, dims.n, dims.r_ctx)
    n_row = rows // tm
    in_specs = [pl.BlockSpec((tm, d), lambda i, k: (i, 0)), _mod_spec(dims, tm, layer),
                pl.BlockSpec((None, None, d, tf), lambda i, k: (j, k, 0, 0)),
                pl.BlockSpec((None, None, d, tf), lambda i, k: (j, k, 0, 0)),
                pl.BlockSpec((None, tf, d), lambda i, k: (j, k, 0)),
                _layer_spec(ln_g, 2 * layer + 1), _layer_spec(ln_b, 2 * layer + 1)]
    out_specs = [pl.BlockSpec((tm, d), lambda i, k: (i, 0))]
    out_shape = [jax.ShapeDtypeStruct((rows, d), F32)]
    cast_in, cast_specs = [], []
    if experts is not None:
        (eg, eu, ed), jj = experts
        _, n_exp, _, dff = eg.shape
        assert dff == nk * tf
        rb = 16
        while n_row * rb < n_exp * d:
            rb *= 2
        nrb = n_exp * d // rb
        rbd = n_exp * tf // nrb
        assert n_exp * d % rb == 0 and n_exp * tf % nrb == 0 and rbd % 16 == 0
        gu_in = pl.BlockSpec((None, rb, tf), lambda i, k: (jj, jnp.minimum(i, nrb - 1), k))
        dn_in = pl.BlockSpec((None, rbd, d), lambda i, k: (jj, jnp.minimum(i, nrb - 1) * nk + k, 0))
        cast_in = [eg.reshape(-1, n_exp * d, dff), eu.reshape(-1, n_exp * d, dff), ed.reshape(-1, n_exp * dff, d)]
        cast_specs = [gu_in, gu_in, dn_in]
        gu_out = pl.BlockSpec((None, rb, tf), lambda i, k: (k, i, 0))
        out_specs += [gu_out, gu_out, pl.BlockSpec((rbd, d), lambda i, k: (i * nk + k, 0))]
        out_shape += [jax.ShapeDtypeStruct((nk, n_row * rb, tf), BF16)] * 2
        out_shape += [jax.ShapeDtypeStruct((n_row * nk * rbd, d), BF16)]
    res = pl.pallas_call(
        functools.partial(_ffn_kernel, alpha=alpha, n_cast=len(cast_in)),
        grid=(n_row, nk),
        in_specs=cast_specs + in_specs,
        out_specs=out_specs,
        out_shape=out_shape,
        scratch_shapes=[pltpu.VMEM((tm, d), BF16), pltpu.VMEM((tm, d), F32)],
        compiler_params=_cparams("arbitrary", "arbitrary"),
        name="dense_ffn",
    )(*cast_in, xs, mods, wg, wu, wd, ln_g, ln_b)
    return res[0], (tuple(res[1:]) if experts is not None else None)


def _inproj_odd_kernel(x_ref, mod_ref, w_ref, up_ref, uf_ref):
    h = (x_ref[...] * (1.0 + mod_ref[0, 1:2, :]) + mod_ref[0, 0:1, :]).astype(BF16)
    pw = up_ref.shape[1]
    up_ref[...] = jnp.dot(h, w_ref[:, 0:pw], preferred_element_type=F32).astype(BF16)
    uf_ref[...] = jnp.dot(h, w_ref[:, pw:], preferred_element_type=F32).astype(BF16)


def _inproj_odd(xs, mods, w_in, dims, rows, pool_width, layer, j):
    d = dims.d
    four_width = w_in.shape[2] - pool_width
    tm = _pick_tile(512, dims.n, dims.r_ctx)
    return pl.pallas_call(
        _inproj_odd_kernel,
        grid=(rows // tm,),
        in_specs=[_row_spec(tm, d), _mod_spec(dims, tm, layer), _layer_spec(w_in, j)],
        out_specs=[_row_spec(tm, pool_width), _row_spec(tm, four_width)],
        out_shape=[jax.ShapeDtypeStruct((rows, pool_width), BF16), jax.ShapeDtypeStruct((rows, four_width), BF16)],
        compiler_params=_cparams("parallel"),
        name="inproj_odd",
    )(xs, mods, w_in)


def _seq_dft_constants(n, group):
    n2 = n // FFT_RADIX
    b = np.arange(n2, dtype=np.float64)
    ka = np.arange(FFT_RADIX, dtype=np.float64)
    ang = 2.0 * np.pi * np.outer(b, ka) / n
    tw = np.concatenate([np.cos(ang), -np.sin(ang)], axis=1)
    ang2 = 2.0 * np.pi * np.outer(b, b) / n2
    cs, sn = np.cos(ang2), np.sin(ang2)
    mat = np.block([[cs, sn], [-sn, cs]]) / np.sqrt(float(n) * group)
    return jnp.asarray(tw, F32), jnp.asarray(mat, BF16)


def _seq_dft_kernel(x_ref, tw_ref, m_ref, or_ref, oi_ref):
    n2 = x_ref.shape[1]
    lanes = x_ref.shape[2]
    for ka in range(FFT_RADIX):
        yr = yi = None
        for a in range(FFT_RADIX):
            ang = 2.0 * np.pi * ((a * ka) % FFT_RADIX) / FFT_RADIX
            cr, ci = float(np.round(np.cos(ang), 12)), float(np.round(-np.sin(ang), 12))
            xa = x_ref[a].astype(F32)
            if cr != 0.0:
                yr = cr * xa if yr is None else yr + cr * xa
            if ci != 0.0:
                yi = ci * xa if yi is None else yi + ci * xa
        if ka == 0:
            zr, zi = yr, jnp.zeros_like(yr)
        else:
            twr = jnp.broadcast_to(tw_ref[:, ka:ka + 1], (n2, lanes))
            twi = jnp.broadcast_to(tw_ref[:, FFT_RADIX + ka:FFT_RADIX + ka + 1], (n2, lanes))
            if yi is None:
                zr, zi = yr * twr, yr * twi
            else:
                zr, zi = yr * twr - yi * twi, yr * twi + yi * twr
        zcat = jnp.concatenate([zr, zi], axis=0).astype(BF16)
        res = jnp.dot(m_ref[...], zcat, preferred_element_type=F32)
        or_ref[ka] = res[0:n2].astype(BF16)
        oi_ref[ka] = res[n2:].astype(BF16)


def _seq_dft(uf, row0, bsz, n, group):
    rows, width = uf.shape
    n2 = n // FFT_RADIX
    lanes = _pick_tile(FFT_LANES, width)
    tw, mat = _seq_dft_constants(n, group)
    x3 = uf.reshape(rows // n2, n2, width)
    slab0 = row0 // n
    out = jax.ShapeDtypeStruct((bsz, FFT_RADIX, n2, width), BF16)
    ospec = pl.BlockSpec((None, FFT_RADIX, n2, lanes), lambda b, c: (b, 0, 0, c))
    return pl.pallas_call(
        _seq_dft_kernel,
        grid=(bsz, width // lanes),
        in_specs=[pl.BlockSpec((FFT_RADIX, n2, lanes), lambda b, c: (slab0 + b, 0, c)),
                  pl.BlockSpec(tw.shape, lambda b, c: (0, 0)),
                  pl.BlockSpec(mat.shape, lambda b, c: (0, 0), pipeline_mode=pl.Buffered(1))],
        out_specs=[ospec, ospec],
        out_shape=[out, out],
        compiler_params=_cparams("parallel", "parallel"),
        name="seq_dft",
    )(x3, tw, mat)


def _top2_routing(logits, n_exp):
    lane = lax.broadcasted_iota(I32, logits.shape, 1)
    lg = jnp.where(lane < n_exp, logits, -jnp.inf)
    m1 = jnp.max(lg, axis=-1, keepdims=True)
    i1 = jnp.min(jnp.where(lg == m1, lane, ROUTE_LANES), axis=-1, keepdims=True)
    lg2 = jnp.where(lane == i1, -jnp.inf, lg)
    m2 = jnp.max(lg2, axis=-1, keepdims=True)
    i2 = jnp.min(jnp.where(lg2 == m2, lane, ROUTE_LANES), axis=-1, keepdims=True)
    e2 = jnp.exp(m2 - m1)
    g1 = 1.0 / (1.0 + e2)
    g2 = e2 * g1
    return jnp.where(lane == 0, i1.astype(F32),
                     jnp.where(lane == 1, i2.astype(F32), jnp.where(lane == 2, g1, jnp.where(lane == 3, g2, 0.0))))


def _outproj_odd_kernel(*refs, dims, alpha, has_ctx, n_exp):
    (up_ref, upp_ref, upn_ref, xr_ref, xi_ref) = refs[:5]
    refs = refs[5:]
    if has_ctx:
        (cr_ref, ci_ref) = refs[:2]
        refs = refs[2:]
    (perm_ref, cs_ref, fw_ref, pw_ref, ps_ref, w_ref, x_ref, mod_ref, g_ref, b_ref, rwh_ref, rwl_ref,
     o_ref, h2_ref, route_ref, ext_ref, cat_ref) = refs
    tm = x_ref.shape[0]
    i = pl.program_id(0)
    row = lax.broadcasted_iota(I32, (tm, 1), 0)
    pos, length = _seq_position(i * tm + row, dims)
    first_pos, _ = _seq_position(i * tm, dims)
    last_pos, last_len = _seq_position(i * tm + tm - 1, dims)

    pool_width = up_ref.shape[1]
    group = pool_width // len(POOL_WINDOWS)
    h0 = POOL_HALO
    ext_ref[0:h0, :] = jnp.where(first_pos == 0, 0.0, upp_ref[HALO_ROWS - h0:HALO_ROWS, :].astype(F32))
    ext_ref[h0:h0 + tm, :] = up_ref[...].astype(F32)
    ext_ref[h0 + tm:h0 + tm + h0, :] = jnp.where(last_pos == last_len - 1, 0.0, upn_ref[0:h0, :].astype(F32))
    for gi, w in enumerate(POOL_WINDOWS):
        cols = slice(gi * group, (gi + 1) * group)
        back, ahead = w // 2, w - w // 2 - 1
        tot = None
        for s in range(-back, ahead + 1):
            t = ext_ref[h0 + s:h0 + s + tm, cols]
            tot = t if tot is None else tot + t
        cnt = jnp.minimum(pos + ahead, length - 1) - jnp.maximum(pos - back, 0) + 1
        pooled = tot / cnt.astype(F32) - ext_ref[h0:h0 + tm, cols]
        yp = jnp.dot(pooled.astype(BF16), pw_ref[gi], preferred_element_type=F32) * ps_ref[:, cols]
        cat_ref[:, cols] = yp.astype(BF16)

    if has_ctx:
        lat = i * tm < dims.r_lat
        xr = jnp.where(lat, xr_ref[...], cr_ref[...])
        xi = jnp.where(lat, xi_ref[...], ci_ref[...])
    else:
        xr, xi = xr_ref[...], xi_ref[...]
    four_width = xr.shape[-1]
    fgroup = four_width // FOURIER_HEADS
    xr = jnp.dot(perm_ref[...], xr.reshape(tm, four_width), preferred_element_type=F32).astype(BF16)
    xi = jnp.dot(perm_ref[...], xi.reshape(tm, four_width), preferred_element_type=F32).astype(BF16)
    for hd in range(FOURIER_HEADS):
        cols = slice(hd * fgroup, (hd + 1) * fgroup)
        both = jnp.concatenate([xr[:, cols], xi[:, cols]], axis=1)
        f = jnp.dot(both, cs_ref[...], preferred_element_type=F32)
        yf = jnp.dot(f.astype(BF16), fw_ref[hd], preferred_element_type=F32)
        cat_ref[:, pool_width + hd * fgroup:pool_width + (hd + 1) * fgroup] = yf.astype(BF16)

    mix = jnp.dot(cat_ref[...], w_ref[...], preferred_element_type=F32)
    xn = _residual_ln(x_ref[...], mix, mod_ref[0, 2:3, :], g_ref[...], b_ref[...], alpha)
    o_ref[...] = xn
    h2 = xn * (1.0 + mod_ref[0, 4:5, :]) + mod_ref[0, 3:4, :]
    h2_ref[...] = h2
    h_hi = h2.astype(BF16)
    h_lo = (h2 - h_hi.astype(F32)).astype(BF16)
    logits = (jnp.dot(h_hi, rwh_ref[...], preferred_element_type=F32)
              + (jnp.dot(h_lo, rwh_ref[...], preferred_element_type=F32)
                 + jnp.dot(h_hi, rwl_ref[...], preferred_element_type=F32)))
    route_ref[...] = _top2_routing(logits, n_exp)


def _channel_dft_matrix(group):
    c = np.arange(group, dtype=np.float64)
    ang = 2.0 * np.pi * np.outer(c, c) / group
    return jnp.asarray(np.concatenate([np.cos(ang), np.sin(ang)], axis=0), BF16)


def _tile_permutation(tm):
    per = tm // FFT_RADIX
    p = np.zeros((tm, tm), np.float32)
    for ka in range(FFT_RADIX):
        for kb in range(per):
            p[kb * FFT_RADIX + ka, ka * per + kb] = 1.0
    return jnp.asarray(p, BF16)


def _outproj_odd(up, lat_ri, ctx_ri, fourier_w, pool_w, pool_scale, w_out, xs, mods, ln_g, ln_b, rw_hi, rw_lo,
                 n_exp, dims, rows, alpha, layer, j):
    d = dims.d
    tm = SEQ_TILE
    pool_width = up.shape[1]
    four_width = lat_ri[0].shape[-1]
    fgroup = four_width // FOURIER_HEADS
    has_ctx = ctx_ri is not None
    per = tm // FFT_RADIX
    lat_tiles = dims.n // tm
    n_lat = dims.r_lat // tm
    upp_spec, upn_spec = _halo_specs(tm, pool_width, up.shape[0])

    def lat_map(i):
        t = jnp.minimum(i, n_lat - 1)
        return (t // lat_tiles, 0, t % lat_tiles, 0)

    lat_spec = pl.BlockSpec((None, FFT_RADIX, per, four_width), lat_map)
    operands = [up, up, up, lat_ri[0], lat_ri[1]]
    specs = [_row_spec(tm, pool_width), upp_spec, upn_spec, lat_spec, lat_spec]
    if has_ctx:
        ctx_tiles = dims.n_ctx // tm

        def ctx_map(i):
            t = jnp.maximum(i - n_lat, 0)
            return (t // ctx_tiles, 0, t % ctx_tiles, 0)

        ctx_spec = pl.BlockSpec((None, FFT_RADIX, per, four_width), ctx_map)
        operands += [ctx_ri[0], ctx_ri[1]]
        specs += [ctx_spec, ctx_spec]
    perm, cs = _tile_permutation(tm), _channel_dft_matrix(fgroup)
    operands += [perm, cs, fourier_w, pool_w, pool_scale, w_out, xs, mods, ln_g, ln_b, rw_hi, rw_lo]
    specs += [_const_spec(perm.shape), _const_spec(cs.shape), _layer_spec(fourier_w, j), _layer_spec(pool_w, j),
              _layer_spec(pool_scale, j), _layer_spec(w_out, layer), _row_spec(tm, d),
              _mod_spec(dims, tm, layer), _layer_spec(ln_g, 2 * layer), _layer_spec(ln_b, 2 * layer),
              _layer_spec(rw_hi, j), _layer_spec(rw_lo, j)]
    return pl.pallas_call(
        functools.partial(_outproj_odd_kernel, dims=dims, alpha=alpha, has_ctx=has_ctx, n_exp=n_exp),
        grid=(rows // tm,),
        in_specs=specs,
        out_specs=[_row_spec(tm, d), _row_spec(tm, d), _row_spec(tm, ROUTE_LANES)],
        out_shape=[jax.ShapeDtypeStruct((rows, d), F32), jax.ShapeDtypeStruct((rows, d), F32),
                   jax.ShapeDtypeStruct((rows, ROUTE_LANES), F32)],
        scratch_shapes=[pltpu.VMEM((tm + 2 * POOL_HALO, pool_width), F32), pltpu.VMEM((tm, d), BF16)],
        compiler_params=_cparams("parallel"),
        name="outproj_odd",
    )(*operands)


def _dispatch_plan(route, n_exp, tile):
    tokens = route.shape[0]
    assign = TOP_K * tokens
    experts = route[:, 0:TOP_K].astype(I32).reshape(assign)
    onehot = (experts[:, None] == jnp.arange(n_exp, dtype=I32)[None, :]).astype(I32)
    running = jnp.cumsum(onehot, axis=0)
    rank = jnp.sum((running - onehot) * onehot, axis=1)
    counts = running[-1]
    padded = ((counts + tile - 1) // tile) * tile
    ends = jnp.cumsum(padded)
    starts = ends - padded
    slot = jnp.sum(onehot * starts[None, :], axis=1) + rank
    n_tiles = -(-(assign + n_exp * (tile - 1)) // tile)
    tile_start = jnp.arange(n_tiles, dtype=I32) * tile
    tile_expert = jnp.minimum(jnp.sum((tile_start[:, None] >= ends[None, :]).astype(I32), axis=1), n_exp - 1)
    live_tiles = ends[-1] // tile
    tile_expert = jnp.where(tile_start < ends[-1], tile_expert, tile_expert[jnp.maximum(live_tiles - 1, 0)])
    pad_first = starts + counts
    pad_head = jnp.minimum((-pad_first) % SUBLANES, padded - counts)
    pad_info = jnp.concatenate([pad_first, pad_head, pad_first + pad_head, padded - counts - pad_head,
                                live_tiles[None]]).astype(I32)
    return slot, pad_info, tile_expert, n_tiles


def _dispatch_kernel(slot_ref, pad_ref, h_ref, o_ref, zero_ref, sem, zsem, *, n_exp, tile, n_tiles, min_tiles):
    i = pl.program_id(0)
    tm = h_ref.shape[0]

    def zero_copy(first_row, size):
        return pltpu.make_async_copy(zero_ref.at[pl.ds(0, size)], o_ref.at[pl.ds(first_row, size)], zsem)

    def padding_copies(act):
        for e in range(n_exp):
            first, head = pad_ref[e], pad_ref[n_exp + e]
            for r in range(SUBLANES - 1):
                pl.when(r < head)(functools.partial(act, zero_copy(first + r, 1)))
            first, length = pad_ref[2 * n_exp + e], pad_ref[3 * n_exp + e]
            for bit in reversed(range(SUBLANES.bit_length() - 1, tile.bit_length() - 1)):
                size = 1 << bit
                for part in range(-(-size // ZERO_ROWS)):
                    rows = min(size, ZERO_ROWS)
                    at = pl.multiple_of(first + part * rows, SUBLANES)
                    pl.when((length & size) != 0)(functools.partial(act, zero_copy(at, rows)))
                first = first + (length & size)
        live = pad_ref[PAD_LIVE * n_exp]
        for t in range(min_tiles, n_tiles):
            for part in range(tile // ZERO_ROWS):
                pl.when(t >= live)(functools.partial(act, zero_copy(t * tile + part * ZERO_ROWS, ZERO_ROWS)))

    @pl.when(i == 0)
    def _():
        zero_ref[...] = jnp.zeros_like(zero_ref)
        padding_copies(lambda cp: cp.start())
        padding_copies(lambda cp: cp.wait())

    base = i * tm * TOP_K

    def issue(r, carry):
        for c in range(TOP_K):
            dst = slot_ref[base + r * TOP_K + c]
            pltpu.make_async_copy(h_ref.at[pl.ds(r, 1)], o_ref.at[pl.ds(dst, 1)], sem).start()
        return carry

    lax.fori_loop(0, tm, issue, 0, unroll=ISSUE_UNROLL)
    for c in range(TOP_K):
        pltpu.make_async_copy(h_ref, o_ref.at[pl.ds(0, tm)], sem).wait()


def _dispatch_rows(slot, pad_info, h2, n_exp, tile, n_tiles):
    tokens, d = h2.shape
    tm = SEQ_TILE
    min_tiles = -(-(TOP_K * tokens) // tile)
    return pl.pallas_call(
        functools.partial(_dispatch_kernel, n_exp=n_exp, tile=tile, n_tiles=n_tiles, min_tiles=min_tiles),
        grid_spec=pltpu.PrefetchScalarGridSpec(
            num_scalar_prefetch=2,
            grid=(tokens // tm,),
            in_specs=[_row_spec(tm, d)],
            out_specs=pl.BlockSpec(memory_space=pl.ANY),
            scratch_shapes=[pltpu.VMEM((ZERO_ROWS, d), h2.dtype), pltpu.SemaphoreType.DMA(()),
                            pltpu.SemaphoreType.DMA(())]),
        out_shape=jax.ShapeDtypeStruct((n_tiles * tile, d), h2.dtype),
        compiler_params=_cparams("arbitrary", unchecked=True),
        name="moe_dispatch",
    )(slot, pad_info, h2)


def _expert_kernel(te_ref, pad_ref, x_ref, wg_ref, wu_ref, wd_ref, o_ref, h_ref, acc_ref, *, n_exp):
    i, k = pl.program_id(0), pl.program_id(1)
    last = pl.num_programs(1) - 1
    live = i < pad_ref[PAD_LIVE * n_exp]

    @pl.when(jnp.logical_and(live, k == 0))
    def _():
        h_ref[...] = x_ref[...].astype(BF16)
        acc_ref[...] = jnp.zeros_like(acc_ref)

    @pl.when(live)
    def _():
        h = h_ref[...]
        gate = jnp.dot(h, wg_ref[...], preferred_element_type=F32)
        up = jnp.dot(h, wu_ref[...], preferred_element_type=F32)
        act = (_silu(gate) * up).astype(BF16)
        acc_ref[...] += jnp.dot(act, wd_ref[...], preferred_element_type=F32)

    @pl.when(jnp.logical_and(live, k == last))
    def _():
        o_ref[...] = acc_ref[...]

    @pl.when(jnp.logical_and(jnp.logical_not(live), k == last))
    def _():
        o_ref[...] = jnp.zeros_like(o_ref)


def _expert_ffn(tile_expert, pad_info, xs, wg, wu, wd, n_exp):
    n_slots, d = xs.shape
    nk, _, tf = wg.shape
    tm = EXPERT_TILE

    def chunk(i, k, pad):
        return jnp.where(i < pad[PAD_LIVE * n_exp], k, nk - 1)

    return pl.pallas_call(
        functools.partial(_expert_kernel, n_exp=n_exp),
        grid_spec=pltpu.PrefetchScalarGridSpec(
            num_scalar_prefetch=2,
            grid=(n_slots // tm, nk),
            in_specs=[pl.BlockSpec((tm, d), lambda i, k, te, pad: (i, 0)),
                      pl.BlockSpec((None, d, tf), lambda i, k, te, pad: (chunk(i, k, pad), te[i], 0)),
                      pl.BlockSpec((None, d, tf), lambda i, k, te, pad: (chunk(i, k, pad), te[i], 0)),
                      pl.BlockSpec((tf, d), lambda i, k, te, pad: (te[i] * nk + chunk(i, k, pad), 0))],
            out_specs=pl.BlockSpec((tm, d), lambda i, k, te, pad: (i, 0)),
            scratch_shapes=[pltpu.VMEM((tm, d), BF16), pltpu.VMEM((tm, d), F32)]),
        out_shape=jax.ShapeDtypeStruct((n_slots, d), F32),
        compiler_params=_cparams("arbitrary", "arbitrary"),
        name="expert_ffn",
    )(tile_expert, pad_info, xs, wg, wu, wd)


def _combine_kernel(slot_ref, y_ref, route_ref, x_ref, mod_ref, g_ref, b_ref, o_ref, buf_ref, sem, *, alpha):
    tm = x_ref.shape[0]
    i = pl.program_id(0)

    def fetch(tile, half):
        base = tile * tm * TOP_K

        def issue(r, carry):
            for c in range(TOP_K):
                src = slot_ref[base + r * TOP_K + c]
                pltpu.make_async_copy(y_ref.at[pl.ds(src, 1)], buf_ref.at[half, c, pl.ds(r, 1)],
                                      sem.at[half]).start()
            return carry

        lax.fori_loop(0, tm, issue, 0, unroll=ISSUE_UNROLL)

    @pl.when(i == 0)
    def _():
        fetch(0, 0)

    @pl.when(i + 1 < pl.num_programs(0))
    def _():
        fetch(i + 1, (i + 1) % 2)

    half = i % 2
    for c in range(TOP_K):
        pltpu.make_async_copy(y_ref.at[pl.ds(0, tm)], buf_ref.at[half, c], sem.at[half]).wait()
    route = route_ref[...]
    ff = route[:, 2:3] * buf_ref[half, 0] + route[:, 3:4] * buf_ref[half, 1]
    o_ref[...] = _residual_ln(x_ref[...], ff, mod_ref[0, 5:6, :], g_ref[...], b_ref[...], alpha)


def _combine(slot, ys, route, xs, mods, ln_g, ln_b, dims, rows, alpha, layer):
    d = dims.d
    tm = SEQ_TILE
    return pl.pallas_call(
        functools.partial(_combine_kernel, alpha=alpha),
        grid_spec=pltpu.PrefetchScalarGridSpec(
            num_scalar_prefetch=1,
            grid=(rows // tm,),
            in_specs=[pl.BlockSpec(memory_space=pl.ANY), _row_spec(tm, ROUTE_LANES), _row_spec(tm, d),
                      _mod_spec(dims, tm, layer), _layer_spec(ln_g, 2 * layer + 1),
                      _layer_spec(ln_b, 2 * layer + 1)],
            out_specs=_row_spec(tm, d),
            scratch_shapes=[pltpu.VMEM((2, TOP_K, tm, d), F32), pltpu.SemaphoreType.DMA((2,))]),
        out_shape=jax.ShapeDtypeStruct((rows, d), F32),
        compiler_params=_cparams("arbitrary", unchecked=True),
        name="moe_combine",
    )(slot, ys, route, xs, mods, ln_g, ln_b)


def _moe(xs, h2, route, mods, wg, wu, wd, n_exp, ln_g, ln_b, dims, rows, alpha, layer):
    slot, pad_info, tile_expert, n_tiles = _dispatch_plan(route, n_exp, EXPERT_TILE)
    sorted_rows = _dispatch_rows(slot, pad_info, h2, n_exp, EXPERT_TILE, n_tiles)
    ys = _expert_ffn(tile_expert, pad_info, sorted_rows, wg, wu, wd, n_exp)
    return _combine(slot, ys, route, xs, mods, ln_g, ln_b, dims, rows, alpha, layer)


def kernel(x, c, ctx, c_ctx, w_mod, b_mod, w_mix_out, ln_g, ln_b, w_in_ab, conv_w, attn_sink, w_in_cd, pool_w,
           pool_scale, fourier_w, ffn_w_gate, ffn_w_up, ffn_w_down, router_w, moe_w_gate, moe_w_up, moe_w_down):
    bsz, n, d = x.shape
    n_ctx = ctx.shape[1]
    depth = w_mod.shape[0]
    n_exp = router_w.shape[2]
    dims = Dims(bsz, n, n_ctx, d)
    assert n % SEQ_TILE == 0 and n_ctx % SEQ_TILE == 0 and n % GRID_W == 0
    assert dims.r_lat % (FFT_RADIX * (n_ctx // FFT_RADIX)) == 0 and dims.r_all % (n // FFT_RADIX) == 0
    alpha = (2.0 * depth) ** 0.25
    pool_width = pool_w.shape[1] * pool_w.shape[2]
    fgroup = fourier_w.shape[2]

    mods = _modulations(c, c_ctx, w_mod, b_mod)
    w_out = w_mix_out.astype(BF16)
    w_ab, w_cd = w_in_ab.astype(BF16), w_in_cd.astype(BF16)
    tf = _pick_tile(512, ffn_w_gate.shape[2], moe_w_gate.shape[3])
    ffn_g, ffn_u, ffn_d = _chunk_major(ffn_w_gate, tf), _chunk_major(ffn_w_up, tf), ffn_w_down.astype(BF16)
    four_w, pool_wb = fourier_w.astype(BF16), pool_w.astype(BF16)
    pool_sc = pool_scale[:, None, :]
    lng, lnb = ln_g.reshape(2 * depth, 1, d), ln_b.reshape(2 * depth, 1, d)
    rw = jnp.zeros(router_w.shape[:2] + (ROUTE_LANES,), F32).at[:, :, :n_exp].set(router_w)
    rw_hi = rw.astype(BF16)
    rw_lo = (rw - rw_hi.astype(F32)).astype(BF16)

    streams = [x.reshape(dims.r_lat, d), ctx.reshape(dims.r_ctx, d)]
    expert_bf = None
    for l in range(depth):
        even = l % 2 == 0
        j = l // 2
        ctx_out = any(m % 2 == 0 for m in range(l + 1, depth))
        rows = dims.r_all if ctx_out else dims.r_lat
        if even:
            q, k, v, z, gb = _inproj_even(streams, mods, w_ab, dims, l, j)
            a = _attention(q, k, v, attn_sink, dims, ctx_out, j)
            xs = _outproj_even(a, z, gb, conv_w, w_out, streams, mods, lng, lnb, dims, rows, alpha, l, j)
            experts = ((moe_w_gate, moe_w_up, moe_w_down), j) if l + 1 < depth else None
            xs, expert_bf = _ffn(xs, mods, ffn_g, ffn_u, ffn_d, lng, lnb, dims, rows, alpha, l, j, experts)
        else:
            xs = streams[0]
            up, uf = _inproj_odd(xs, mods, w_cd, dims, rows, pool_width, l, j)
            lat_ri = _seq_dft(uf, 0, bsz, n, fgroup)
            ctx_ri = _seq_dft(uf, dims.r_lat, bsz, n_ctx, fgroup) if ctx_out else None
            xs, h2, route = _outproj_odd(up, lat_ri, ctx_ri, four_w, pool_wb, pool_sc, w_out, xs, mods, lng, lnb,
                                         rw_hi, rw_lo, n_exp, dims, rows, alpha, l, j)
            xs = _moe(xs, h2, route, mods, *expert_bf, n_exp, lng, lnb, dims, rows, alpha, l)
        streams = [xs]
    return xs[:dims.r_lat].reshape(bsz, n, d)
```

```python
import functools
from typing import NamedTuple

import numpy as np
import jax
import jax.numpy as jnp
from jax import lax
from jax.experimental import pallas as pl
from jax.experimental.pallas import tpu as pltpu

F32 = jnp.float32
BF16 = jnp.bfloat16
I32 = jnp.int32

HEAD_DIM = 128
N_Q_HEADS = 8
N_KV_HEADS = 2
Q_PER_KV = N_Q_HEADS // N_KV_HEADS
ATTN_WIDTH = N_Q_HEADS * HEAD_DIM
KV_WIDTH = N_KV_HEADS * HEAD_DIM
ATTN_BLOCK = 128
GRID_W = 64
ROPE_THETA = 10000.0
NEG_INF = -1e30
POOL_WINDOWS = (2, 4, 8, 16)
POOL_HALO = 8
HALO_ROWS = 16
FOURIER_HEADS = 4
N_MOD = 6
LN_EPS = 1e-5
TOP_K = 2
FFT_RADIX = 8
FFT_LANES = 256
SEQ_TILE = 256
ROUTE_LANES = 128
EXPERT_TILE = 512
WEIGHT_SLOTS = 3
ISSUE_UNROLL = 8
ZERO_ROWS = 256
SUBLANES = 8
PAD_LIVE = 4
V7X_VMEM_LIMIT = 56 * 1024 * 1024


class Dims(NamedTuple):
    bsz: int
    n: int
    n_ctx: int
    d: int

    @property
    def r_lat(self):
        return self.bsz * self.n

    @property
    def r_ctx(self):
        return self.bsz * self.n_ctx

    @property
    def r_all(self):
        return self.r_lat + self.r_ctx


def _pick_tile(pref, *dims):
    t = pref
    while any(d % t for d in dims):
        t //= 2
    return t


def _cparams(*sem, unchecked=False):
    return pltpu.CompilerParams(dimension_semantics=sem, vmem_limit_bytes=V7X_VMEM_LIMIT,
                                disable_bounds_checks=unchecked)


def _residual_ln(x, y, gate, g, b, alpha):
    v = alpha * x + gate * y
    mu = jnp.mean(v, axis=-1, keepdims=True)
    vc = v - mu
    var = jnp.mean(vc * vc, axis=-1, keepdims=True)
    return vc * lax.rsqrt(var + LN_EPS) * g + b


def _silu(t):
    return t * jax.nn.sigmoid(t)


def _seq_position(row, dims):
    lat = row < dims.r_lat
    pos = jnp.where(lat, lax.rem(row, dims.n), lax.rem(row - dims.r_lat, dims.n_ctx))
    length = jnp.where(lat, dims.n, dims.n_ctx)
    return pos, length


def _mod_spec(dims, tm, layer):
    return pl.BlockSpec((None, 1, N_MOD, dims.d), lambda i, *_: (layer, (i * tm) // dims.n, 0, 0))


def _row_spec(tm, width):
    return pl.BlockSpec((tm, width), lambda i, *_: (i, 0))


def _stream_specs(streams, tm, dims):
    if len(streams) == 1:
        return [_row_spec(tm, dims.d)]
    n_lat = dims.r_lat // tm
    return [pl.BlockSpec((tm, dims.d), lambda i, *_: (jnp.minimum(i, n_lat - 1), 0)),
            pl.BlockSpec((tm, dims.d), lambda i, *_: (jnp.maximum(i - n_lat, 0), 0))]


def _stream_rows(refs, dims):
    if len(refs) == 1:
        return refs[0][...]
    tm = refs[0].shape[0]
    return jnp.where(pl.program_id(0) * tm < dims.r_lat, refs[0][...], refs[1][...])


def _layer_spec(arr, index, **kw):
    zeros = (0,) * (arr.ndim - 1)
    return pl.BlockSpec((None,) + arr.shape[1:], lambda *_: (index,) + zeros, **kw)


def _const_spec(shape):
    zeros = (0,) * len(shape)
    return pl.BlockSpec(shape, lambda *_: zeros)


def _mod_kernel(cond_ref, w_ref, b_ref, o_ref):
    s = _silu(cond_ref[...]).astype(BF16)
    o_ref[...] = jnp.dot(s, w_ref[...].astype(BF16), preferred_element_type=F32) + b_ref[...]


def _modulations(c, c_ctx, w_mod, b_mod):
    depth, d, nmod = w_mod.shape
    bsz = c.shape[0]
    rows = -(-(bsz + 1) // 8) * 8
    cond = jnp.zeros((rows, d), F32).at[:bsz].set(c).at[bsz].set(c_ctx)
    tn = _pick_tile(1024, nmod)
    out = pl.pallas_call(
        _mod_kernel,
        grid=(depth, nmod // tn),
        in_specs=[pl.BlockSpec((rows, d), lambda l, j: (0, 0)),
                  pl.BlockSpec((None, d, tn), lambda l, j: (l, 0, j)),
                  pl.BlockSpec((None, 1, tn), lambda l, j: (l, 0, j))],
        out_specs=pl.BlockSpec((None, rows, tn), lambda l, j: (l, 0, j)),
        out_shape=jax.ShapeDtypeStruct((depth, rows, nmod), F32),
        compiler_params=_cparams("arbitrary", "arbitrary"),
        name="modulation",
    )(cond, w_mod, b_mod.reshape(depth, 1, nmod))
    return out.reshape(depth, rows, N_MOD, d)


def _rope_tables(n, tm):
    rows = n // GRID_W
    row = jnp.repeat(jnp.arange(rows, dtype=F32), GRID_W)
    col = jnp.tile(jnp.arange(GRID_W, dtype=F32), rows)
    half = HEAD_DIM // 2
    inv = ROPE_THETA ** (-jnp.arange(0, half, 2, dtype=F32) / half)
    ang_r = row[:, None] * inv
    ang_c = col[:, None] * inv
    ang = jnp.concatenate([ang_r, ang_r, ang_c, ang_c], -1)
    cos, sin = jnp.cos(ang), jnp.sin(ang)
    quarter = HEAD_DIM // 4
    first = (jnp.arange(HEAD_DIM) // quarter) % 2 == 0
    sin_up = jnp.where(first, -sin, 0.0)
    sin_dn = jnp.where(first, 0.0, sin)
    ident = jnp.zeros((tm, HEAD_DIM), F32)
    return (jnp.concatenate([cos, ident + 1.0], 0), jnp.concatenate([sin_up, ident], 0),
            jnp.concatenate([sin_dn, ident], 0))


def _inproj_even_kernel(*refs, conv_width, dims, n_streams):
    x_refs, refs = refs[:n_streams], refs[n_streams:]
    mod_ref, w_ref, cos_ref, sup_ref, sdn_ref, q_ref, k_ref, v_ref, z_ref, gb_ref = refs
    h = (_stream_rows(x_refs, dims) * (1.0 + mod_ref[0, 1:2, :]) + mod_ref[0, 0:1, :]).astype(BF16)
    cos, sup, sdn = cos_ref[...], sup_ref[...], sdn_ref[...]
    quarter = HEAD_DIM // 4

    def rope(u):
        return u * cos + pltpu.roll(u, HEAD_DIM - quarter, 1) * sup + pltpu.roll(u, quarter, 1) * sdn

    q = jnp.dot(h, w_ref[:, 0:ATTN_WIDTH], preferred_element_type=F32)
    for hd in range(N_Q_HEADS):
        sl = slice(hd * HEAD_DIM, (hd + 1) * HEAD_DIM)
        q_ref[:, sl] = rope(q[:, sl]).astype(BF16)
    o = ATTN_WIDTH
    kv = jnp.dot(h, w_ref[:, o:o + 2 * KV_WIDTH], preferred_element_type=F32)
    for hd in range(N_KV_HEADS):
        sl = slice(hd * HEAD_DIM, (hd + 1) * HEAD_DIM)
        k_ref[:, sl] = rope(kv[:, sl]).astype(BF16)
    v_ref[...] = kv[:, KV_WIDTH:].astype(BF16)
    o += 2 * KV_WIDTH
    u = jnp.dot(h, w_ref[:, o:o + conv_width], preferred_element_type=F32)
    gc = jnp.dot(h, w_ref[:, o + 2 * conv_width:o + 3 * conv_width], preferred_element_type=F32)
    z_ref[...] = (gc * u).astype(BF16)
    gb = jnp.dot(h, w_ref[:, o + conv_width:o + 2 * conv_width], preferred_element_type=F32)
    gb_ref[...] = gb.astype(BF16)


def _inproj_even(streams, mods, w_in, dims, layer, j):
    rows = dims.r_all
    conv_width = (w_in.shape[2] - ATTN_WIDTH - 2 * KV_WIDTH) // 3
    tm = _pick_tile(512, dims.n, dims.r_ctx)
    cos, sup, sdn = _rope_tables(dims.n, tm)
    nlat = dims.n // tm
    tab_spec = pl.BlockSpec((tm, HEAD_DIM), lambda i: (jnp.where(i * tm < dims.r_lat, i % nlat, nlat), 0))
    return pl.pallas_call(
        functools.partial(_inproj_even_kernel, conv_width=conv_width, dims=dims, n_streams=len(streams)),
        grid=(rows // tm,),
        in_specs=_stream_specs(streams, tm, dims)
        + [_mod_spec(dims, tm, layer), _layer_spec(w_in, j, pipeline_mode=pl.Buffered(1)),
           tab_spec, tab_spec, tab_spec],
        out_specs=[_row_spec(tm, ATTN_WIDTH), _row_spec(tm, KV_WIDTH), _row_spec(tm, KV_WIDTH),
                   _row_spec(tm, conv_width), _row_spec(tm, conv_width)],
        out_shape=[jax.ShapeDtypeStruct((rows, ATTN_WIDTH), BF16), jax.ShapeDtypeStruct((rows, KV_WIDTH), BF16),
                   jax.ShapeDtypeStruct((rows, KV_WIDTH), BF16), jax.ShapeDtypeStruct((rows, conv_width), BF16),
                   jax.ShapeDtypeStruct((rows, conv_width), BF16)],
        compiler_params=_cparams("parallel"),
        name="inproj_even",
    )(*streams, mods, w_in, cos, sup, sdn)


def _attn_kernel(sink_ref, q_ref, kp_ref, kc_ref, kn_ref, kx_ref, vp_ref, vc_ref, vn_ref, vx_ref, o_ref, *,
                 nb, n_ctx, j):
    i = pl.program_id(1)
    blk = ATTN_BLOCK
    rows = Q_PER_KV * blk
    latent = i < nb
    has_prev = jnp.logical_and(latent, i > 0)
    has_next = jnp.logical_and(latent, i < nb - 1)
    qi = lax.broadcasted_iota(I32, (blk, blk), 0)
    kj = lax.broadcasted_iota(I32, (blk, blk), 1)
    bias_prev = jnp.where(jnp.logical_and(kj >= qi, has_prev), 0.0, NEG_INF).astype(F32)
    bias_next = jnp.where(jnp.logical_and(kj <= qi, has_next), 0.0, NEG_INF).astype(F32)
    bias_prev = jnp.concatenate([bias_prev] * Q_PER_KV, axis=0)
    bias_next = jnp.concatenate([bias_next] * Q_PER_KV, axis=0)
    bias_own = jnp.where(latent, 0.0, NEG_INF).astype(F32)
    grp = lax.broadcasted_iota(I32, (rows, 1), 0) // blk
    scale = HEAD_DIM ** -0.5
    for hk in range(N_KV_HEADS):
        sl = slice(hk * HEAD_DIM, (hk + 1) * HEAD_DIM)
        kcat = jnp.concatenate([kx_ref[:, sl], kp_ref[:, sl], kc_ref[:, sl], kn_ref[:, sl]], axis=0)
        vcat = jnp.concatenate([vx_ref[:, sl], vp_ref[:, sl], vc_ref[:, sl], vn_ref[:, sl]], axis=0)
        heads = [hk * Q_PER_KV + g for g in range(Q_PER_KV)]
        qs = jnp.concatenate([q_ref[:, hq * HEAD_DIM:(hq + 1) * HEAD_DIM] for hq in heads], axis=0)
        s = lax.dot_general(qs, kcat, (((1,), (1,)), ((), ())), preferred_element_type=F32) * scale
        s = jnp.concatenate([s[:, 0:n_ctx], s[:, n_ctx:n_ctx + blk] + bias_prev,
                             s[:, n_ctx + blk:n_ctx + 2 * blk] + bias_own, s[:, n_ctx + 2 * blk:] + bias_next], axis=1)
        sink = jnp.full((rows, 1), sink_ref[j, heads[-1]], F32)
        for g in range(Q_PER_KV - 1):
            sink = jnp.where(grp == g, sink_ref[j, heads[g]], sink)
        m = jnp.maximum(jnp.max(s, axis=-1, keepdims=True), sink)
        p = jnp.exp(s - m)
        den = jnp.sum(p, axis=-1, keepdims=True) + jnp.exp(sink - m)
        pn = (p * (1.0 / den)).astype(BF16)
        o = jnp.dot(pn, vcat, preferred_element_type=F32)
        for g, hq in enumerate(heads):
            o_ref[:, hq * HEAD_DIM:(hq + 1) * HEAD_DIM] = o[g * blk:(g + 1) * blk].astype(BF16)


def _attention(q, k, v, sink, dims, ctx_out, j):
    blk = ATTN_BLOCK
    nb = dims.n // blk
    nqc = dims.n_ctx // blk if ctx_out else 0
    rows_out = dims.r_all if ctx_out else dims.r_lat
    lat_blocks = dims.r_lat // blk

    def q_map(b, i, *_):
        return (jnp.where(i < nb, b * nb + i, lat_blocks + b * nqc + (i - nb)), 0)

    def k_map(shift):
        return lambda b, i, *_: (b * nb + jnp.clip(i + shift, 0, nb - 1), 0)

    def ctx_map(b, i, *_):
        return (dims.r_lat // dims.n_ctx + b, 0)

    kv_specs = [pl.BlockSpec((blk, KV_WIDTH), k_map(-1)), pl.BlockSpec((blk, KV_WIDTH), k_map(0)),
                pl.BlockSpec((blk, KV_WIDTH), k_map(1)), pl.BlockSpec((dims.n_ctx, KV_WIDTH), ctx_map)]
    return pl.pallas_call(
        functools.partial(_attn_kernel, nb=nb, n_ctx=dims.n_ctx, j=j),
        grid=(dims.bsz, nb + nqc),
        in_specs=[pl.BlockSpec(memory_space=pltpu.SMEM), pl.BlockSpec((blk, ATTN_WIDTH), q_map)]
        + kv_specs + kv_specs,
        out_specs=pl.BlockSpec((blk, ATTN_WIDTH), q_map),
        out_shape=jax.ShapeDtypeStruct((rows_out, ATTN_WIDTH), BF16),
        compiler_params=_cparams("parallel", "parallel"),
        name="band_attention",
    )(sink, q, k, k, k, k, v, v, v, v)


def _halo_specs(tm, width, total_rows):
    per = tm // HALO_ROWS
    last = total_rows // HALO_ROWS - 1
    prev = pl.BlockSpec((HALO_ROWS, width), lambda i, *_: (jnp.maximum(i * per - 1, 0), 0))
    nxt = pl.BlockSpec((HALO_ROWS, width), lambda i, *_: (jnp.minimum((i + 1) * per, last), 0))
    return prev, nxt


def _outproj_even_kernel(*refs, dims, alpha, n_streams):
    x_refs, refs = refs[:n_streams], refs[n_streams:]
    a_ref, z_ref, zp_ref, zn_ref, gb_ref, cw_ref, w_ref, mod_ref, g_ref, b_ref, o_ref, cat_ref = refs
    tm = o_ref.shape[0]
    row = lax.broadcasted_iota(I32, (tm, 1), 0)
    pos, length = _seq_position(pl.program_id(0) * tm + row, dims)
    z = z_ref[...].astype(F32)
    zprev = jnp.where(row == 0, zp_ref[HALO_ROWS - 1:HALO_ROWS, :].astype(F32), pltpu.roll(z, 1, 0))
    zprev = jnp.where(pos == 0, 0.0, zprev)
    znext = jnp.where(row == tm - 1, zn_ref[0:1, :].astype(F32), pltpu.roll(z, tm - 1, 0))
    znext = jnp.where(pos == length - 1, 0.0, znext)
    y = zprev * cw_ref[0:1, :] + z * cw_ref[1:2, :] + znext * cw_ref[2:3, :]
    cat_ref[:, 0:ATTN_WIDTH] = a_ref[...]
    cat_ref[:, ATTN_WIDTH:] = (gb_ref[...].astype(F32) * y).astype(BF16)
    mix = jnp.dot(cat_ref[...], w_ref[...], preferred_element_type=F32)
    o_ref[...] = _residual_ln(_stream_rows(x_refs, dims), mix, mod_ref[0, 2:3, :], g_ref[...], b_ref[...], alpha)


def _outproj_even(a, z, gb, conv_w, w_out, streams, mods, ln_g, ln_b, dims, rows, alpha, layer, j):
    d = dims.d
    conv_width = z.shape[1]
    tm = _pick_tile(512, dims.n, dims.r_ctx)
    zp_spec, zn_spec = _halo_specs(tm, conv_width, z.shape[0])
    return pl.pallas_call(
        functools.partial(_outproj_even_kernel, dims=dims, alpha=alpha, n_streams=len(streams)),
        grid=(rows // tm,),
        in_specs=_stream_specs(streams, tm, dims)
        + [_row_spec(tm, ATTN_WIDTH), _row_spec(tm, conv_width), zp_spec, zn_spec,
           _row_spec(tm, conv_width), _layer_spec(conv_w, j), _layer_spec(w_out, layer),
           _mod_spec(dims, tm, layer), _layer_spec(ln_g, 2 * layer), _layer_spec(ln_b, 2 * layer)],
        out_specs=_row_spec(tm, d),
        out_shape=jax.ShapeDtypeStruct((rows, d), F32),
        scratch_shapes=[pltpu.VMEM((tm, d), BF16)],
        compiler_params=_cparams("parallel"),
        name="outproj_even",
    )(*streams, a, z, z, z, gb, conv_w, w_out, mods, ln_g, ln_b)


def _ffn_kernel(*refs, alpha, n_cast):
    casts_in, refs = refs[:n_cast], refs[n_cast:]
    x_ref, mod_ref, wg_ref, wu_ref, wd_ref, g_ref, b_ref, o_ref = refs[:8]
    casts_out, (h_ref, acc_ref) = refs[8:8 + n_cast], refs[8 + n_cast:]
    for src_ref, dst_ref in zip(casts_in, casts_out):
        dst_ref[...] = src_ref[...].astype(BF16)
    k = pl.program_id(1)

    @pl.when(k == 0)
    def _():
        h_ref[...] = (x_ref[...] * (1.0 + mod_ref[0, 4:5, :]) + mod_ref[0, 3:4, :]).astype(BF16)
        acc_ref[...] = jnp.zeros_like(acc_ref)

    h = h_ref[...]
    gate = jnp.dot(h, wg_ref[...], preferred_element_type=F32)
    up = jnp.dot(h, wu_ref[...], preferred_element_type=F32)
    act = (_silu(gate) * up).astype(BF16)
    acc_ref[...] += jnp.dot(act, wd_ref[...], preferred_element_type=F32)

    @pl.when(k == pl.num_programs(1) - 1)
    def _():
        o_ref[...] = _residual_ln(x_ref[...], acc_ref[...], mod_ref[0, 5:6, :], g_ref[...], b_ref[...], alpha)


def _chunk_major(w, tf):
    layers, d, dff = w.shape
    return w.astype(BF16).reshape(layers, d, dff // tf, tf).transpose(0, 2, 1, 3)


def _ffn(xs, mods, wg, wu, wd, ln_g, ln_b, dims, rows, alpha, layer, j, experts=None):
    _, nk, d, tf = wg.shape
    tm = _pick_tile(512, dims.n, dims.r_ctx)
    n_row = rows // tm
    in_specs = [pl.BlockSpec((tm, d), lambda i, k: (i, 0)), _mod_spec(dims, tm, layer),
                pl.BlockSpec((None, None, d, tf), lambda i, k: (j, k, 0, 0)),
                pl.BlockSpec((None, None, d, tf), lambda i, k: (j, k, 0, 0)),
                pl.BlockSpec((None, tf, d), lambda i, k: (j, k, 0)),
                _layer_spec(ln_g, 2 * layer + 1), _layer_spec(ln_b, 2 * layer + 1)]
    out_specs = [pl.BlockSpec((tm, d), lambda i, k: (i, 0))]
    out_shape = [jax.ShapeDtypeStruct((rows, d), F32)]
    cast_in, cast_specs = [], []
    if experts is not None:
        (eg, eu, ed), jj = experts
        _, n_exp, _, dff = eg.shape
        assert dff == nk * tf
        rb = 16
        while n_row * rb < n_exp * d:
            rb *= 2
        nrb = n_exp * d // rb
        rbd = n_exp * tf // nrb
        assert n_exp * d % rb == 0 and n_exp * tf % nrb == 0 and rbd % 16 == 0
        gu_in = pl.BlockSpec((None, rb, tf), lambda i, k: (jj, jnp.minimum(i, nrb - 1), k))
        dn_in = pl.BlockSpec((None, rbd, d), lambda i, k: (jj, jnp.minimum(i, nrb - 1) * nk + k, 0))
        cast_in = [eg.reshape(-1, n_exp * d, dff), eu.reshape(-1, n_exp * d, dff), ed.reshape(-1, n_exp * dff, d)]
        cast_specs = [gu_in, gu_in, dn_in]
        gu_out = pl.BlockSpec((None, rb, tf), lambda i, k: (k, i, 0))
        out_specs += [gu_out, gu_out, pl.BlockSpec((rbd, d), lambda i, k: (i * nk + k, 0))]
        out_shape += [jax.ShapeDtypeStruct((nk, n_row * rb, tf), BF16)] * 2
        out_shape += [jax.ShapeDtypeStruct((n_row * nk * rbd, d), BF16)]
    res = pl.pallas_call(
        functools.partial(_ffn_kernel, alpha=alpha, n_cast=len(cast_in)),
        grid=(n_row, nk),
        in_specs=cast_specs + in_specs,
        out_specs=out_specs,
        out_shape=out_shape,
        scratch_shapes=[pltpu.VMEM((tm, d), BF16), pltpu.VMEM((tm, d), F32)],
        compiler_params=_cparams("arbitrary", "arbitrary"),
        name="dense_ffn",
    )(*cast_in, xs, mods, wg, wu, wd, ln_g, ln_b)
    return res[0], (tuple(res[1:]) if experts is not None else None)


def _inproj_odd_kernel(x_ref, mod_ref, w_ref, up_ref, uf_ref):
    h = (x_ref[...] * (1.0 + mod_ref[0, 1:2, :]) + mod_ref[0, 0:1, :]).astype(BF16)
    pw = up_ref.shape[1]
    up_ref[...] = jnp.dot(h, w_ref[:, 0:pw], preferred_element_type=F32).astype(BF16)
    uf_ref[...] = jnp.dot(h, w_ref[:, pw:], preferred_element_type=F32).astype(BF16)


def _inproj_odd(xs, mods, w_in, dims, rows, pool_width, layer, j):
    d = dims.d
    four_width = w_in.shape[2] - pool_width
    tm = _pick_tile(512, dims.n, dims.r_ctx)
    return pl.pallas_call(
        _inproj_odd_kernel,
        grid=(rows // tm,),
        in_specs=[_row_spec(tm, d), _mod_spec(dims, tm, layer), _layer_spec(w_in, j)],
        out_specs=[_row_spec(tm, pool_width), _row_spec(tm, four_width)],
        out_shape=[jax.ShapeDtypeStruct((rows, pool_width), BF16), jax.ShapeDtypeStruct((rows, four_width), BF16)],
        compiler_params=_cparams("parallel"),
        name="inproj_odd",
    )(xs, mods, w_in)


def _seq_dft_constants(n, group):
    n2 = n // FFT_RADIX
    b = np.arange(n2, dtype=np.float64)
    ka = np.arange(FFT_RADIX, dtype=np.float64)
    ang = 2.0 * np.pi * np.outer(b, ka) / n
    tw = np.concatenate([np.cos(ang), -np.sin(ang)], axis=1)
    ang2 = 2.0 * np.pi * np.outer(b, b) / n2
    cs, sn = np.cos(ang2), np.sin(ang2)
    mat = np.block([[cs, sn], [-sn, cs]]) / np.sqrt(float(n) * group)
    return jnp.asarray(tw, F32), jnp.asarray(mat, BF16)


def _seq_dft_kernel(x_ref, tw_ref, m_ref, or_ref, oi_ref):
    n2 = x_ref.shape[1]
    lanes = x_ref.shape[2]
    for ka in range(FFT_RADIX):
        yr = yi = None
        for a in range(FFT_RADIX):
            ang = 2.0 * np.pi * ((a * ka) % FFT_RADIX) / FFT_RADIX
            cr, ci = float(np.round(np.cos(ang), 12)), float(np.round(-np.sin(ang), 12))
            xa = x_ref[a].astype(F32)
            if cr != 0.0:
                yr = cr * xa if yr is None else yr + cr * xa
            if ci != 0.0:
                yi = ci * xa if yi is None else yi + ci * xa
        if ka == 0:
            zr, zi = yr, jnp.zeros_like(yr)
        else:
            twr = jnp.broadcast_to(tw_ref[:, ka:ka + 1], (n2, lanes))
            twi = jnp.broadcast_to(tw_ref[:, FFT_RADIX + ka:FFT_RADIX + ka + 1], (n2, lanes))
            if yi is None:
                zr, zi = yr * twr, yr * twi
            else:
                zr, zi = yr * twr - yi * twi, yr * twi + yi * twr
        zcat = jnp.concatenate([zr, zi], axis=0).astype(BF16)
        res = jnp.dot(m_ref[...], zcat, preferred_element_type=F32)
        or_ref[ka] = res[0:n2].astype(BF16)
        oi_ref[ka] = res[n2:].astype(BF16)


def _seq_dft(uf, row0, bsz, n, group):
    rows, width = uf.shape
    n2 = n // FFT_RADIX
    lanes = _pick_tile(FFT_LANES, width)
    tw, mat = _seq_dft_constants(n, group)
    x3 = uf.reshape(rows // n2, n2, width)
    slab0 = row0 // n
    out = jax.ShapeDtypeStruct((bsz, FFT_RADIX, n2, width), BF16)
    ospec = pl.BlockSpec((None, FFT_RADIX, n2, lanes), lambda b, c: (b, 0, 0, c))
    return pl.pallas_call(
        _seq_dft_kernel,
        grid=(bsz, width // lanes),
        in_specs=[pl.BlockSpec((FFT_RADIX, n2, lanes), lambda b, c: (slab0 + b, 0, c)),
                  pl.BlockSpec(tw.shape, lambda b, c: (0, 0)),
                  pl.BlockSpec(mat.shape, lambda b, c: (0, 0), pipeline_mode=pl.Buffered(1))],
        out_specs=[ospec, ospec],
        out_shape=[out, out],
        compiler_params=_cparams("parallel", "parallel"),
        name="seq_dft",
    )(x3, tw, mat)


def _top2_routing(logits, n_exp):
    lane = lax.broadcasted_iota(I32, logits.shape, 1)
    lg = jnp.where(lane < n_exp, logits, -jnp.inf)
    m1 = jnp.max(lg, axis=-1, keepdims=True)
    i1 = jnp.min(jnp.where(lg == m1, lane, ROUTE_LANES), axis=-1, keepdims=True)
    lg2 = jnp.where(lane == i1, -jnp.inf, lg)
    m2 = jnp.max(lg2, axis=-1, keepdims=True)
    i2 = jnp.min(jnp.where(lg2 == m2, lane, ROUTE_LANES), axis=-1, keepdims=True)
    e2 = jnp.exp(m2 - m1)
    g1 = 1.0 / (1.0 + e2)
    g2 = e2 * g1
    return jnp.where(lane == 0, i1.astype(F32),
                     jnp.where(lane == 1, i2.astype(F32), jnp.where(lane == 2, g1, jnp.where(lane == 3, g2, 0.0))))


def _outproj_odd_kernel(*refs, dims, alpha, has_ctx, n_exp):
    (up_ref, upp_ref, upn_ref, xr_ref, xi_ref) = refs[:5]
    refs = refs[5:]
    if has_ctx:
        (cr_ref, ci_ref) = refs[:2]
        refs = refs[2:]
    (perm_ref, cs_ref, fw_ref, pw_ref, ps_ref, w_ref, x_ref, mod_ref, g_ref, b_ref, rwh_ref, rwl_ref,
     o_ref, h2_ref, route_ref, ext_ref, cat_ref) = refs
    tm = x_ref.shape[0]
    i = pl.program_id(0)
    row = lax.broadcasted_iota(I32, (tm, 1), 0)
    pos, length = _seq_position(i * tm + row, dims)
    first_pos, _ = _seq_position(i * tm, dims)
    last_pos, last_len = _seq_position(i * tm + tm - 1, dims)

    pool_width = up_ref.shape[1]
    group = pool_width // len(POOL_WINDOWS)
    h0 = POOL_HALO
    ext_ref[0:h0, :] = jnp.where(first_pos == 0, 0.0, upp_ref[HALO_ROWS - h0:HALO_ROWS, :].astype(F32))
    ext_ref[h0:h0 + tm, :] = up_ref[...].astype(F32)
    ext_ref[h0 + tm:h0 + tm + h0, :] = jnp.where(last_pos == last_len - 1, 0.0, upn_ref[0:h0, :].astype(F32))
    for gi, w in enumerate(POOL_WINDOWS):
        cols = slice(gi * group, (gi + 1) * group)
        back, ahead = w // 2, w - w // 2 - 1
        tot = None
        for s in range(-back, ahead + 1):
            t = ext_ref[h0 + s:h0 + s + tm, cols]
            tot = t if tot is None else tot + t
        cnt = jnp.minimum(pos + ahead, length - 1) - jnp.maximum(pos - back, 0) + 1
        pooled = tot / cnt.astype(F32) - ext_ref[h0:h0 + tm, cols]
        yp = jnp.dot(pooled.astype(BF16), pw_ref[gi], preferred_element_type=F32) * ps_ref[:, cols]
        cat_ref[:, cols] = yp.astype(BF16)

    if has_ctx:
        lat = i * tm < dims.r_lat
        xr = jnp.where(lat, xr_ref[...], cr_ref[...])
        xi = jnp.where(lat, xi_ref[...], ci_ref[...])
    else:
        xr, xi = xr_ref[...], xi_ref[...]
    four_width = xr.shape[-1]
    fgroup = four_width // FOURIER_HEADS
    xr = jnp.dot(perm_ref[...], xr.reshape(tm, four_width), preferred_element_type=F32).astype(BF16)
    xi = jnp.dot(perm_ref[...], xi.reshape(tm, four_width), preferred_element_type=F32).astype(BF16)
    for hd in range(FOURIER_HEADS):
        cols = slice(hd * fgroup, (hd + 1) * fgroup)
        both = jnp.concatenate([xr[:, cols], xi[:, cols]], axis=1)
        f = jnp.dot(both, cs_ref[...], preferred_element_type=F32)
        yf = jnp.dot(f.astype(BF16), fw_ref[hd], preferred_element_type=F32)
        cat_ref[:, pool_width + hd * fgroup:pool_width + (hd + 1) * fgroup] = yf.astype(BF16)

    mix = jnp.dot(cat_ref[...], w_ref[...], preferred_element_type=F32)
    xn = _residual_ln(x_ref[...], mix, mod_ref[0, 2:3, :], g_ref[...], b_ref[...], alpha)
    o_ref[...] = xn
    h2 = xn * (1.0 + mod_ref[0, 4:5, :]) + mod_ref[0, 3:4, :]
    h2_ref[...] = h2
    h_hi = h2.astype(BF16)
    h_lo = (h2 - h_hi.astype(F32)).astype(BF16)
    logits = (jnp.dot(h_hi, rwh_ref[...], preferred_element_type=F32)
              + (jnp.dot(h_lo, rwh_ref[...], preferred_element_type=F32)
                 + jnp.dot(h_hi, rwl_ref[...], preferred_element_type=F32)))
    route_ref[...] = _top2_routing(logits, n_exp)


def _channel_dft_matrix(group):
    c = np.arange(group, dtype=np.float64)
    ang = 2.0 * np.pi * np.outer(c, c) / group
    return jnp.asarray(np.concatenate([np.cos(ang), np.sin(ang)], axis=0), BF16)


def _tile_permutation(tm):
    per = tm // FFT_RADIX
    p = np.zeros((tm, tm), np.float32)
    for ka in range(FFT_RADIX):
        for kb in range(per):
            p[kb * FFT_RADIX + ka, ka * per + kb] = 1.0
    return jnp.asarray(p, BF16)


def _outproj_odd(up, lat_ri, ctx_ri, fourier_w, pool_w, pool_scale, w_out, xs, mods, ln_g, ln_b, rw_hi, rw_lo,
                 n_exp, dims, rows, alpha, layer, j):
    d = dims.d
    tm = SEQ_TILE
    pool_width = up.shape[1]
    four_width = lat_ri[0].shape[-1]
    fgroup = four_width // FOURIER_HEADS
    has_ctx = ctx_ri is not None
    per = tm // FFT_RADIX
    lat_tiles = dims.n // tm
    n_lat = dims.r_lat // tm
    upp_spec, upn_spec = _halo_specs(tm, pool_width, up.shape[0])

    def lat_map(i):
        t = jnp.minimum(i, n_lat - 1)
        return (t // lat_tiles, 0, t % lat_tiles, 0)

    lat_spec = pl.BlockSpec((None, FFT_RADIX, per, four_width), lat_map)
    operands = [up, up, up, lat_ri[0], lat_ri[1]]
    specs = [_row_spec(tm, pool_width), upp_spec, upn_spec, lat_spec, lat_spec]
    if has_ctx:
        ctx_tiles = dims.n_ctx // tm

        def ctx_map(i):
            t = jnp.maximum(i - n_lat, 0)
            return (t // ctx_tiles, 0, t % ctx_tiles, 0)

        ctx_spec = pl.BlockSpec((None, FFT_RADIX, per, four_width), ctx_map)
        operands += [ctx_ri[0], ctx_ri[1]]
        specs += [ctx_spec, ctx_spec]
    perm, cs = _tile_permutation(tm), _channel_dft_matrix(fgroup)
    operands += [perm, cs, fourier_w, pool_w, pool_scale, w_out, xs, mods, ln_g, ln_b, rw_hi, rw_lo]
    specs += [_const_spec(perm.shape), _const_spec(cs.shape), _layer_spec(fourier_w, j), _layer_spec(pool_w, j),
              _layer_spec(pool_scale, j), _layer_spec(w_out, layer), _row_spec(tm, d),
              _mod_spec(dims, tm, layer), _layer_spec(ln_g, 2 * layer), _layer_spec(ln_b, 2 * layer),
              _layer_spec(rw_hi, j), _layer_spec(rw_lo, j)]
    return pl.pallas_call(
        functools.partial(_outproj_odd_kernel, dims=dims, alpha=alpha, has_ctx=has_ctx, n_exp=n_exp),
        grid=(rows // tm,),
        in_specs=specs,
        out_specs=[_row_spec(tm, d), _row_spec(tm, d), _row_spec(tm, ROUTE_LANES)],
        out_shape=[jax.ShapeDtypeStruct((rows, d), F32), jax.ShapeDtypeStruct((rows, d), F32),
                   jax.ShapeDtypeStruct((rows, ROUTE_LANES), F32)],
        scratch_shapes=[pltpu.VMEM((tm + 2 * POOL_HALO, pool_width), F32), pltpu.VMEM((tm, d), BF16)],
        compiler_params=_cparams("parallel"),
        name="outproj_odd",
    )(*operands)


def _dispatch_plan(route, n_exp, tile):
    tokens = route.shape[0]
    assign = TOP_K * tokens
    experts = route[:, 0:TOP_K].astype(I32).reshape(assign)
    onehot = (experts[:, None] == jnp.arange(n_exp, dtype=I32)[None, :]).astype(I32)
    running = jnp.cumsum(onehot, axis=0)
    rank = jnp.sum((running - onehot) * onehot, axis=1)
    counts = running[-1]
    padded = ((counts + tile - 1) // tile) * tile
    ends = jnp.cumsum(padded)
    starts = ends - padded
    slot = jnp.sum(onehot * starts[None, :], axis=1) + rank
    n_tiles = -(-(assign + n_exp * (tile - 1)) // tile)
    tile_start = jnp.arange(n_tiles, dtype=I32) * tile
    tile_expert = jnp.minimum(jnp.sum((tile_start[:, None] >= ends[None, :]).astype(I32), axis=1), n_exp - 1)
    live_tiles = ends[-1] // tile
    tile_expert = jnp.where(tile_start < ends[-1], tile_expert, tile_expert[jnp.maximum(live_tiles - 1, 0)])
    pad_first = starts + counts
    pad_head = jnp.minimum((-pad_first) % SUBLANES, padded - counts)
    pad_info = jnp.concatenate([pad_first, pad_head, pad_first + pad_head, padded - counts - pad_head,
                                live_tiles[None]]).astype(I32)
    return slot, pad_info, tile_expert, n_tiles


def _dispatch_kernel(slot_ref, pad_ref, h_ref, o_ref, zero_ref, sem, zsem, *, n_exp, tile, n_tiles, min_tiles):
    i = pl.program_id(0)
    tm = h_ref.shape[0]

    def zero_copy(first_row, size):
        return pltpu.make_async_copy(zero_ref.at[pl.ds(0, size)], o_ref.at[pl.ds(first_row, size)], zsem)

    def padding_copies(act):
        for e in range(n_exp):
            first, head = pad_ref[e], pad_ref[n_exp + e]
            for r in range(SUBLANES - 1):
                pl.when(r < head)(functools.partial(act, zero_copy(first + r, 1)))
            first, length = pad_ref[2 * n_exp + e], pad_ref[3 * n_exp + e]
            for bit in reversed(range(SUBLANES.bit_length() - 1, tile.bit_length() - 1)):
                size = 1 << bit
                for part in range(-(-size // ZERO_ROWS)):
                    rows = min(size, ZERO_ROWS)
                    at = pl.multiple_of(first + part * rows, SUBLANES)
                    pl.when((length & size) != 0)(functools.partial(act, zero_copy(at, rows)))
                first = first + (length & size)
        live = pad_ref[PAD_LIVE * n_exp]
        for t in range(min_tiles, n_tiles):
            for part in range(tile // ZERO_ROWS):
                pl.when(t >= live)(functools.partial(act, zero_copy(t * tile + part * ZERO_ROWS, ZERO_ROWS)))

    @pl.when(i == 0)
    def _():
        zero_ref[...] = jnp.zeros_like(zero_ref)
        padding_copies(lambda cp: cp.start())
        padding_copies(lambda cp: cp.wait())

    base = i * tm * TOP_K

    def issue(r, carry):
        for c in range(TOP_K):
            dst = slot_ref[base + r * TOP_K + c]
            pltpu.make_async_copy(h_ref.at[pl.ds(r, 1)], o_ref.at[pl.ds(dst, 1)], sem).start()
        return carry

    lax.fori_loop(0, tm, issue, 0, unroll=ISSUE_UNROLL)
    for c in range(TOP_K):
        pltpu.make_async_copy(h_ref, o_ref.at[pl.ds(0, tm)], sem).wait()


def _dispatch_rows(slot, pad_info, h2, n_exp, tile, n_tiles):
    tokens, d = h2.shape
    tm = SEQ_TILE
    min_tiles = -(-(TOP_K * tokens) // tile)
    return pl.pallas_call(
        functools.partial(_dispatch_kernel, n_exp=n_exp, tile=tile, n_tiles=n_tiles, min_tiles=min_tiles),
        grid_spec=pltpu.PrefetchScalarGridSpec(
            num_scalar_prefetch=2,
            grid=(tokens // tm,),
            in_specs=[_row_spec(tm, d)],
            out_specs=pl.BlockSpec(memory_space=pl.ANY),
            scratch_shapes=[pltpu.VMEM((ZERO_ROWS, d), h2.dtype), pltpu.SemaphoreType.DMA(()),
                            pltpu.SemaphoreType.DMA(())]),
        out_shape=jax.ShapeDtypeStruct((n_tiles * tile, d), h2.dtype),
        compiler_params=_cparams("arbitrary", unchecked=True),
        name="moe_dispatch",
    )(slot, pad_info, h2)


def _expert_kernel(te_ref, pad_ref, x_ref, wg_ref, wu_ref, wd_ref, o_ref, wgb_ref, wub_ref, wdb_ref, h_ref, acc_ref,
                   sem, *, n_exp, nk):
    i = pl.program_id(0)
    live_tiles = pad_ref[PAD_LIVE * n_exp]
    d = x_ref.shape[1]
    tf = wgb_ref.shape[2]

    def chunk_copies(expert, c, slot):
        rows = pl.ds(pl.multiple_of(expert * d, d), d)
        down = pl.ds(pl.multiple_of((expert * nk + c) * tf, tf), tf)
        return (pltpu.make_async_copy(wg_ref.at[c, rows, :], wgb_ref.at[slot], sem.at[0, slot]),
                pltpu.make_async_copy(wu_ref.at[c, rows, :], wub_ref.at[slot], sem.at[1, slot]),
                pltpu.make_async_copy(wd_ref.at[down, :], wdb_ref.at[slot], sem.at[2, slot]))

    @pl.when(i < live_tiles)
    def _():
        expert = te_ref[i]

        @pl.when(i == 0)
        def _():
            for cp in chunk_copies(expert, 0, 0):
                cp.start()

        h_ref[...] = x_ref[...].astype(BF16)
        for c in range(nk):
            slot = c % WEIGHT_SLOTS
            for cp in chunk_copies(expert, c, slot):
                cp.wait()
            if c + 1 < nk:
                for cp in chunk_copies(expert, c + 1, (c + 1) % WEIGHT_SLOTS):
                    cp.start()
            else:
                @pl.when(i + 1 < live_tiles)
                def _():
                    for cp in chunk_copies(te_ref[i + 1], 0, 0):
                        cp.start()
            h = h_ref[...]
            gate = jnp.dot(h, wgb_ref[slot], preferred_element_type=F32)
            up = jnp.dot(h, wub_ref[slot], preferred_element_type=F32)
            act = (_silu(gate) * up).astype(BF16)
            part = jnp.dot(act, wdb_ref[slot], preferred_element_type=F32)
            if c == 0:
                acc_ref[...] = part
            else:
                acc_ref[...] += part
        o_ref[...] = acc_ref[...]

    @pl.when(i >= live_tiles)
    def _():
        o_ref[...] = jnp.zeros_like(o_ref)


def _expert_ffn(tile_expert, pad_info, xs, wg, wu, wd, n_exp):
    n_slots, d = xs.shape
    nk, _, tf = wg.shape
    tm = EXPERT_TILE
    assert (nk - 1) % WEIGHT_SLOTS != 0
    return pl.pallas_call(
        functools.partial(_expert_kernel, n_exp=n_exp, nk=nk),
        grid_spec=pltpu.PrefetchScalarGridSpec(
            num_scalar_prefetch=2,
            grid=(n_slots // tm,),
            in_specs=[pl.BlockSpec((tm, d), lambda i, te, pad: (i, 0)),
                      pl.BlockSpec(memory_space=pl.ANY), pl.BlockSpec(memory_space=pl.ANY),
                      pl.BlockSpec(memory_space=pl.ANY)],
            out_specs=pl.BlockSpec((tm, d), lambda i, te, pad: (i, 0)),
            scratch_shapes=[pltpu.VMEM((WEIGHT_SLOTS, d, tf), BF16), pltpu.VMEM((WEIGHT_SLOTS, d, tf), BF16),
                            pltpu.VMEM((WEIGHT_SLOTS, tf, d), BF16), pltpu.VMEM((tm, d), BF16),
                            pltpu.VMEM((tm, d), F32), pltpu.SemaphoreType.DMA((3, WEIGHT_SLOTS))]),
        out_shape=jax.ShapeDtypeStruct((n_slots, d), F32),
        compiler_params=_cparams("arbitrary"),
        name="expert_ffn",
    )(tile_expert, pad_info, xs, wg, wu, wd)


def _combine_kernel(slot_ref, y_ref, route_ref, x_ref, mod_ref, g_ref, b_ref, o_ref, buf_ref, sem, *, alpha):
    tm = x_ref.shape[0]
    i = pl.program_id(0)

    def fetch(tile, half):
        base = tile * tm * TOP_K

        def issue(r, carry):
            for c in range(TOP_K):
                src = slot_ref[base + r * TOP_K + c]
                pltpu.make_async_copy(y_ref.at[pl.ds(src, 1)], buf_ref.at[half, c, pl.ds(r, 1)],
                                      sem.at[half]).start()
            return carry

        lax.fori_loop(0, tm, issue, 0, unroll=ISSUE_UNROLL)

    @pl.when(i == 0)
    def _():
        fetch(0, 0)

    @pl.when(i + 1 < pl.num_programs(0))
    def _():
        fetch(i + 1, (i + 1) % 2)

    half = i % 2
    for c in range(TOP_K):
        pltpu.make_async_copy(y_ref.at[pl.ds(0, tm)], buf_ref.at[half, c], sem.at[half]).wait()
    route = route_ref[...]
    ff = route[:, 2:3] * buf_ref[half, 0] + route[:, 3:4] * buf_ref[half, 1]
    o_ref[...] = _residual_ln(x_ref[...], ff, mod_ref[0, 5:6, :], g_ref[...], b_ref[...], alpha)


def _combine(slot, ys, route, xs, mods, ln_g, ln_b, dims, rows, alpha, layer):
    d = dims.d
    tm = SEQ_TILE
    return pl.pallas_call(
        functools.partial(_combine_kernel, alpha=alpha),
        grid_spec=pltpu.PrefetchScalarGridSpec(
            num_scalar_prefetch=1,
            grid=(rows // tm,),
            in_specs=[pl.BlockSpec(memory_space=pl.ANY), _row_spec(tm, ROUTE_LANES), _row_spec(tm, d),
                      _mod_spec(dims, tm, layer), _layer_spec(ln_g, 2 * layer + 1),
                      _layer_spec(ln_b, 2 * layer + 1)],
            out_specs=_row_spec(tm, d),
            scratch_shapes=[pltpu.VMEM((2, TOP_K, tm, d), F32), pltpu.SemaphoreType.DMA((2,))]),
        out_shape=jax.ShapeDtypeStruct((rows, d), F32),
        compiler_params=_cparams("arbitrary", unchecked=True),
        name="moe_combine",
    )(slot, ys, route, xs, mods, ln_g, ln_b)


def _moe(xs, h2, route, mods, wg, wu, wd, n_exp, ln_g, ln_b, dims, rows, alpha, layer):
    slot, pad_info, tile_expert, n_tiles = _dispatch_plan(route, n_exp, EXPERT_TILE)
    sorted_rows = _dispatch_rows(slot, pad_info, h2, n_exp, EXPERT_TILE, n_tiles)
    ys = _expert_ffn(tile_expert, pad_info, sorted_rows, wg, wu, wd, n_exp)
    return _combine(slot, ys, route, xs, mods, ln_g, ln_b, dims, rows, alpha, layer)


def kernel(x, c, ctx, c_ctx, w_mod, b_mod, w_mix_out, ln_g, ln_b, w_in_ab, conv_w, attn_sink, w_in_cd, pool_w,
           pool_scale, fourier_w, ffn_w_gate, ffn_w_up, ffn_w_down, router_w, moe_w_gate, moe_w_up, moe_w_down):
    bsz, n, d = x.shape
    n_ctx = ctx.shape[1]
    depth = w_mod.shape[0]
    n_exp = router_w.shape[2]
    dims = Dims(bsz, n, n_ctx, d)
    assert n % SEQ_TILE == 0 and n_ctx % SEQ_TILE == 0 and n % GRID_W == 0
    assert dims.r_lat % (FFT_RADIX * (n_ctx // FFT_RADIX)) == 0 and dims.r_all % (n // FFT_RADIX) == 0
    alpha = (2.0 * depth) ** 0.25
    pool_width = pool_w.shape[1] * pool_w.shape[2]
    fgroup = fourier_w.shape[2]

    mods = _modulations(c, c_ctx, w_mod, b_mod)
    w_out = w_mix_out.astype(BF16)
    w_ab, w_cd = w_in_ab.astype(BF16), w_in_cd.astype(BF16)
    tf = _pick_tile(512, ffn_w_gate.shape[2], moe_w_gate.shape[3])
    ffn_g, ffn_u, ffn_d = _chunk_major(ffn_w_gate, tf), _chunk_major(ffn_w_up, tf), ffn_w_down.astype(BF16)
    four_w, pool_wb = fourier_w.astype(BF16), pool_w.astype(BF16)
    pool_sc = pool_scale[:, None, :]
    lng, lnb = ln_g.reshape(2 * depth, 1, d), ln_b.reshape(2 * depth, 1, d)
    rw = jnp.zeros(router_w.shape[:2] + (ROUTE_LANES,), F32).at[:, :, :n_exp].set(router_w)
    rw_hi = rw.astype(BF16)
    rw_lo = (rw - rw_hi.astype(F32)).astype(BF16)

    streams = [x.reshape(dims.r_lat, d), ctx.reshape(dims.r_ctx, d)]
    expert_bf = None
    for l in range(depth):
        even = l % 2 == 0
        j = l // 2
        ctx_out = any(m % 2 == 0 for m in range(l + 1, depth))
        rows = dims.r_all if ctx_out else dims.r_lat
        if even:
            q, k, v, z, gb = _inproj_even(streams, mods, w_ab, dims, l, j)
            a = _attention(q, k, v, attn_sink, dims, ctx_out, j)
            xs = _outproj_even(a, z, gb, conv_w, w_out, streams, mods, lng, lnb, dims, rows, alpha, l, j)
            experts = ((moe_w_gate, moe_w_up, moe_w_down), j) if l + 1 < depth else None
            xs, expert_bf = _ffn(xs, mods, ffn_g, ffn_u, ffn_d, lng, lnb, dims, rows, alpha, l, j, experts)
        else:
            xs = streams[0]
            up, uf = _inproj_odd(xs, mods, w_cd, dims, rows, pool_width, l, j)
            lat_ri = _seq_dft(uf, 0, bsz, n, fgroup)
            ctx_ri = _seq_dft(uf, dims.r_lat, bsz, n_ctx, fgroup) if ctx_out else None
            xs, h2, route = _outproj_odd(up, lat_ri, ctx_ri, four_w, pool_wb, pool_sc, w_out, xs, mods, lng, lnb,
                                         rw_hi, rw_lo, n_exp, dims, rows, alpha, l, j)
            xs = _moe(xs, h2, route, mods, *expert_bf, n_exp, lng, lnb, dims, rows, alpha, l)
        streams = [xs]
    return xs[:dims.r_lat].reshape(bsz, n, d)
```

```python
import functools
from typing import NamedTuple

import numpy as np
import jax
import jax.numpy as jnp
from jax import lax
from jax.experimental import pallas as pl
from jax.experimental.pallas import tpu as pltpu

F32 = jnp.float32
BF16 = jnp.bfloat16
I32 = jnp.int32

HEAD_DIM = 128
N_Q_HEADS = 8
N_KV_HEADS = 2
Q_PER_KV = N_Q_HEADS // N_KV_HEADS
ATTN_WIDTH = N_Q_HEADS * HEAD_DIM
KV_WIDTH = N_KV_HEADS * HEAD_DIM
ATTN_BLOCK = 128
GRID_W = 64
ROPE_THETA = 10000.0
NEG_INF = -1e30
POOL_WINDOWS = (2, 4, 8, 16)
POOL_HALO = 8
HALO_ROWS = 16
FOURIER_HEADS = 4
N_MOD = 6
LN_EPS = 1e-5
TOP_K = 2
FFT_RADIX = 8
FFT_LANES = 256
SEQ_TILE = 256
ROUTE_LANES = 128
EXPERT_TILE = 512
WEIGHT_SLOTS = 3
ISSUE_UNROLL = 8
ZERO_ROWS = 256
SUBLANES = 8
PAD_LIVE = 4
V7X_VMEM_LIMIT = 56 * 1024 * 1024


class Dims(NamedTuple):
    bsz: int
    n: int
    n_ctx: int
    d: int

    @property
    def r_lat(self):
        return self.bsz * self.n

    @property
    def r_ctx(self):
        return self.bsz * self.n_ctx

    @property
    def r_all(self):
        return self.r_lat + self.r_ctx


def _pick_tile(pref, *dims):
    t = pref
    while any(d % t for d in dims):
        t //= 2
    return t


def _cparams(*sem, unchecked=False):
    return pltpu.CompilerParams(dimension_semantics=sem, vmem_limit_bytes=V7X_VMEM_LIMIT,
                                disable_bounds_checks=unchecked)


def _residual_ln(x, y, gate, g, b, alpha):
    v = alpha * x + gate * y
    mu = jnp.mean(v, axis=-1, keepdims=True)
    vc = v - mu
    var = jnp.mean(vc * vc, axis=-1, keepdims=True)
    return vc * lax.rsqrt(var + LN_EPS) * g + b


def _silu(t):
    return t * jax.nn.sigmoid(t)


def _seq_position(row, dims):
    lat = row < dims.r_lat
    pos = jnp.where(lat, lax.rem(row, dims.n), lax.rem(row - dims.r_lat, dims.n_ctx))
    length = jnp.where(lat, dims.n, dims.n_ctx)
    return pos, length


def _mod_spec(dims, tm, layer):
    return pl.BlockSpec((None, 1, N_MOD, dims.d), lambda i, *_: (layer, (i * tm) // dims.n, 0, 0))


def _row_spec(tm, width):
    return pl.BlockSpec((tm, width), lambda i, *_: (i, 0))


def _stream_specs(streams, tm, dims):
    if len(streams) == 1:
        return [_row_spec(tm, dims.d)]
    n_lat = dims.r_lat // tm
    return [pl.BlockSpec((tm, dims.d), lambda i, *_: (jnp.minimum(i, n_lat - 1), 0)),
            pl.BlockSpec((tm, dims.d), lambda i, *_: (jnp.maximum(i - n_lat, 0), 0))]


def _stream_rows(refs, dims):
    if len(refs) == 1:
        return refs[0][...]
    tm = refs[0].shape[0]
    return jnp.where(pl.program_id(0) * tm < dims.r_lat, refs[0][...], refs[1][...])


def _layer_spec(arr, index, **kw):
    zeros = (0,) * (arr.ndim - 1)
    return pl.BlockSpec((None,) + arr.shape[1:], lambda *_: (index,) + zeros, **kw)


def _const_spec(shape):
    zeros = (0,) * len(shape)
    return pl.BlockSpec(shape, lambda *_: zeros)


def _mod_kernel(cond_ref, w_ref, b_ref, o_ref):
    s = _silu(cond_ref[...]).astype(BF16)
    o_ref[...] = jnp.dot(s, w_ref[...].astype(BF16), preferred_element_type=F32) + b_ref[...]


def _modulations(c, c_ctx, w_mod, b_mod):
    depth, d, nmod = w_mod.shape
    bsz = c.shape[0]
    rows = -(-(bsz + 1) // 8) * 8
    cond = jnp.zeros((rows, d), F32).at[:bsz].set(c).at[bsz].set(c_ctx)
    tn = _pick_tile(1024, nmod)
    out = pl.pallas_call(
        _mod_kernel,
        grid=(depth, nmod // tn),
        in_specs=[pl.BlockSpec((rows, d), lambda l, j: (0, 0)),
                  pl.BlockSpec((None, d, tn), lambda l, j: (l, 0, j)),
                  pl.BlockSpec((None, 1, tn), lambda l, j: (l, 0, j))],
        out_specs=pl.BlockSpec((None, rows, tn), lambda l, j: (l, 0, j)),
        out_shape=jax.ShapeDtypeStruct((depth, rows, nmod), F32),
        compiler_params=_cparams("arbitrary", "arbitrary"),
        name="modulation",
    )(cond, w_mod, b_mod.reshape(depth, 1, nmod))
    return out.reshape(depth, rows, N_MOD, d)


def _rope_tables(n, tm):
    rows = n // GRID_W
    row = jnp.repeat(jnp.arange(rows, dtype=F32), GRID_W)
    col = jnp.tile(jnp.arange(GRID_W, dtype=F32), rows)
    half = HEAD_DIM // 2
    inv = ROPE_THETA ** (-jnp.arange(0, half, 2, dtype=F32) / half)
    ang_r = row[:, None] * inv
    ang_c = col[:, None] * inv
    ang = jnp.concatenate([ang_r, ang_r, ang_c, ang_c], -1)
    cos, sin = jnp.cos(ang), jnp.sin(ang)
    quarter = HEAD_DIM // 4
    first = (jnp.arange(HEAD_DIM) // quarter) % 2 == 0
    sin_up = jnp.where(first, -sin, 0.0)
    sin_dn = jnp.where(first, 0.0, sin)
    ident = jnp.zeros((tm, HEAD_DIM), F32)
    return (jnp.concatenate([cos, ident + 1.0], 0), jnp.concatenate([sin_up, ident], 0),
            jnp.concatenate([sin_dn, ident], 0))


def _inproj_even_kernel(*refs, conv_width, dims, n_streams):
    x_refs, refs = refs[:n_streams], refs[n_streams:]
    mod_ref, w_ref, cos_ref, sup_ref, sdn_ref, q_ref, k_ref, v_ref, z_ref, gb_ref = refs
    h = (_stream_rows(x_refs, dims) * (1.0 + mod_ref[0, 1:2, :]) + mod_ref[0, 0:1, :]).astype(BF16)
    cos, sup, sdn = cos_ref[...], sup_ref[...], sdn_ref[...]
    quarter = HEAD_DIM // 4

    def rope(u):
        return u * cos + pltpu.roll(u, HEAD_DIM - quarter, 1) * sup + pltpu.roll(u, quarter, 1) * sdn

    q = jnp.dot(h, w_ref[:, 0:ATTN_WIDTH], preferred_element_type=F32)
    for hd in range(N_Q_HEADS):
        sl = slice(hd * HEAD_DIM, (hd + 1) * HEAD_DIM)
        q_ref[:, sl] = rope(q[:, sl]).astype(BF16)
    o = ATTN_WIDTH
    kv = jnp.dot(h, w_ref[:, o:o + 2 * KV_WIDTH], preferred_element_type=F32)
    for hd in range(N_KV_HEADS):
        sl = slice(hd * HEAD_DIM, (hd + 1) * HEAD_DIM)
        k_ref[:, sl] = rope(kv[:, sl]).astype(BF16)
    v_ref[...] = kv[:, KV_WIDTH:].astype(BF16)
    o += 2 * KV_WIDTH
    u = jnp.dot(h, w_ref[:, o:o + conv_width], preferred_element_type=F32)
    gc = jnp.dot(h, w_ref[:, o + 2 * conv_width:o + 3 * conv_width], preferred_element_type=F32)
    z_ref[...] = (gc * u).astype(BF16)
    gb = jnp.dot(h, w_ref[:, o + conv_width:o + 2 * conv_width], preferred_element_type=F32)
    gb_ref[...] = gb.astype(BF16)


def _inproj_even(streams, mods, w_in, dims, layer, j):
    rows = dims.r_all
    conv_width = (w_in.shape[2] - ATTN_WIDTH - 2 * KV_WIDTH) // 3
    tm = _pick_tile(512, dims.n, dims.r_ctx)
    cos, sup, sdn = _rope_tables(dims.n, tm)
    nlat = dims.n // tm
    tab_spec = pl.BlockSpec((tm, HEAD_DIM), lambda i: (jnp.where(i * tm < dims.r_lat, i % nlat, nlat), 0))
    return pl.pallas_call(
        functools.partial(_inproj_even_kernel, conv_width=conv_width, dims=dims, n_streams=len(streams)),
        grid=(rows // tm,),
        in_specs=_stream_specs(streams, tm, dims)
        + [_mod_spec(dims, tm, layer), _layer_spec(w_in, j, pipeline_mode=pl.Buffered(1)),
           tab_spec, tab_spec, tab_spec],
        out_specs=[_row_spec(tm, ATTN_WIDTH), _row_spec(tm, KV_WIDTH), _row_spec(tm, KV_WIDTH),
                   _row_spec(tm, conv_width), _row_spec(tm, conv_width)],
        out_shape=[jax.ShapeDtypeStruct((rows, ATTN_WIDTH), BF16), jax.ShapeDtypeStruct((rows, KV_WIDTH), BF16),
                   jax.ShapeDtypeStruct((rows, KV_WIDTH), BF16), jax.ShapeDtypeStruct((rows, conv_width), BF16),
                   jax.ShapeDtypeStruct((rows, conv_width), BF16)],
        compiler_params=_cparams("parallel"),
        name="inproj_even",
    )(*streams, mods, w_in, cos, sup, sdn)


def _attn_kernel(sink_ref, q_ref, kp_ref, kc_ref, kn_ref, kx_ref, vp_ref, vc_ref, vn_ref, vx_ref, o_ref, *,
                 nb, n_ctx, j):
    i = pl.program_id(1)
    blk = ATTN_BLOCK
    rows = Q_PER_KV * blk
    latent = i < nb
    has_prev = jnp.logical_and(latent, i > 0)
    has_next = jnp.logical_and(latent, i < nb - 1)
    qi = lax.broadcasted_iota(I32, (blk, blk), 0)
    kj = lax.broadcasted_iota(I32, (blk, blk), 1)
    bias_prev = jnp.where(jnp.logical_and(kj >= qi, has_prev), 0.0, NEG_INF).astype(F32)
    bias_next = jnp.where(jnp.logical_and(kj <= qi, has_next), 0.0, NEG_INF).astype(F32)
    bias_prev = jnp.concatenate([bias_prev] * Q_PER_KV, axis=0)
    bias_next = jnp.concatenate([bias_next] * Q_PER_KV, axis=0)
    bias_own = jnp.where(latent, 0.0, NEG_INF).astype(F32)
    grp = lax.broadcasted_iota(I32, (rows, 1), 0) // blk
    scale = HEAD_DIM ** -0.5
    for hk in range(N_KV_HEADS):
        sl = slice(hk * HEAD_DIM, (hk + 1) * HEAD_DIM)
        kcat = jnp.concatenate([kx_ref[:, sl], kp_ref[:, sl], kc_ref[:, sl], kn_ref[:, sl]], axis=0)
        vcat = jnp.concatenate([vx_ref[:, sl], vp_ref[:, sl], vc_ref[:, sl], vn_ref[:, sl]], axis=0)
        heads = [hk * Q_PER_KV + g for g in range(Q_PER_KV)]
        qs = jnp.concatenate([q_ref[:, hq * HEAD_DIM:(hq + 1) * HEAD_DIM] for hq in heads], axis=0)
        s = lax.dot_general(qs, kcat, (((1,), (1,)), ((), ())), preferred_element_type=F32) * scale
        s = jnp.concatenate([s[:, 0:n_ctx], s[:, n_ctx:n_ctx + blk] + bias_prev,
                             s[:, n_ctx + blk:n_ctx + 2 * blk] + bias_own, s[:, n_ctx + 2 * blk:] + bias_next], axis=1)
        sink = jnp.full((rows, 1), sink_ref[j, heads[-1]], F32)
        for g in range(Q_PER_KV - 1):
            sink = jnp.where(grp == g, sink_ref[j, heads[g]], sink)
        m = jnp.maximum(jnp.max(s, axis=-1, keepdims=True), sink)
        p = jnp.exp(s - m)
        den = jnp.sum(p, axis=-1, keepdims=True) + jnp.exp(sink - m)
        pn = (p * (1.0 / den)).astype(BF16)
        o = jnp.dot(pn, vcat, preferred_element_type=F32)
        for g, hq in enumerate(heads):
            o_ref[:, hq * HEAD_DIM:(hq + 1) * HEAD_DIM] = o[g * blk:(g + 1) * blk].astype(BF16)


def _attention(q, k, v, sink, dims, ctx_out, j):
    blk = ATTN_BLOCK
    nb = dims.n // blk
    nqc = dims.n_ctx // blk if ctx_out else 0
    rows_out = dims.r_all if ctx_out else dims.r_lat
    lat_blocks = dims.r_lat // blk

    def q_map(b, i, *_):
        return (jnp.where(i < nb, b * nb + i, lat_blocks + b * nqc + (i - nb)), 0)

    def k_map(shift):
        return lambda b, i, *_: (b * nb + jnp.clip(i + shift, 0, nb - 1), 0)

    def ctx_map(b, i, *_):
        return (dims.r_lat // dims.n_ctx + b, 0)

    kv_specs = [pl.BlockSpec((blk, KV_WIDTH), k_map(-1)), pl.BlockSpec((blk, KV_WIDTH), k_map(0)),
                pl.BlockSpec((blk, KV_WIDTH), k_map(1)), pl.BlockSpec((dims.n_ctx, KV_WIDTH), ctx_map)]
    return pl.pallas_call(
        functools.partial(_attn_kernel, nb=nb, n_ctx=dims.n_ctx, j=j),
        grid=(dims.bsz, nb + nqc),
        in_specs=[pl.BlockSpec(memory_space=pltpu.SMEM), pl.BlockSpec((blk, ATTN_WIDTH), q_map)]
        + kv_specs + kv_specs,
        out_specs=pl.BlockSpec((blk, ATTN_WIDTH), q_map),
        out_shape=jax.ShapeDtypeStruct((rows_out, ATTN_WIDTH), BF16),
        compiler_params=_cparams("parallel", "parallel"),
        name="band_attention",
    )(sink, q, k, k, k, k, v, v, v, v)


def _halo_specs(tm, width, total_rows):
    per = tm // HALO_ROWS
    last = total_rows // HALO_ROWS - 1
    prev = pl.BlockSpec((HALO_ROWS, width), lambda i, *_: (jnp.maximum(i * per - 1, 0), 0))
    nxt = pl.BlockSpec((HALO_ROWS, width), lambda i, *_: (jnp.minimum((i + 1) * per, last), 0))
    return prev, nxt


def _outproj_even_kernel(*refs, dims, alpha, n_streams):
    x_refs, refs = refs[:n_streams], refs[n_streams:]
    a_ref, z_ref, zp_ref, zn_ref, gb_ref, cw_ref, w_ref, mod_ref, g_ref, b_ref, o_ref, cat_ref = refs
    tm = o_ref.shape[0]
    row = lax.broadcasted_iota(I32, (tm, 1), 0)
    pos, length = _seq_position(pl.program_id(0) * tm + row, dims)
    z = z_ref[...].astype(F32)
    zprev = jnp.where(row == 0, zp_ref[HALO_ROWS - 1:HALO_ROWS, :].astype(F32), pltpu.roll(z, 1, 0))
    zprev = jnp.where(pos == 0, 0.0, zprev)
    znext = jnp.where(row == tm - 1, zn_ref[0:1, :].astype(F32), pltpu.roll(z, tm - 1, 0))
    znext = jnp.where(pos == length - 1, 0.0, znext)
    y = zprev * cw_ref[0:1, :] + z * cw_ref[1:2, :] + znext * cw_ref[2:3, :]
    cat_ref[:, 0:ATTN_WIDTH] = a_ref[...]
    cat_ref[:, ATTN_WIDTH:] = (gb_ref[...].astype(F32) * y).astype(BF16)
    mix = jnp.dot(cat_ref[...], w_ref[...], preferred_element_type=F32)
    o_ref[...] = _residual_ln(_stream_rows(x_refs, dims), mix, mod_ref[0, 2:3, :], g_ref[...], b_ref[...], alpha)


def _outproj_even(a, z, gb, conv_w, w_out, streams, mods, ln_g, ln_b, dims, rows, alpha, layer, j):
    d = dims.d
    conv_width = z.shape[1]
    tm = _pick_tile(512, dims.n, dims.r_ctx)
    zp_spec, zn_spec = _halo_specs(tm, conv_width, z.shape[0])
    return pl.pallas_call(
        functools.partial(_outproj_even_kernel, dims=dims, alpha=alpha, n_streams=len(streams)),
        grid=(rows // tm,),
        in_specs=_stream_specs(streams, tm, dims)
        + [_row_spec(tm, ATTN_WIDTH), _row_spec(tm, conv_width), zp_spec, zn_spec,
           _row_spec(tm, conv_width), _layer_spec(conv_w, j), _layer_spec(w_out, layer),
           _mod_spec(dims, tm, layer), _layer_spec(ln_g, 2 * layer), _layer_spec(ln_b, 2 * layer)],
        out_specs=_row_spec(tm, d),
        out_shape=jax.ShapeDtypeStruct((rows, d), F32),
        scratch_shapes=[pltpu.VMEM((tm, d), BF16)],
        compiler_params=_cparams("parallel"),
        name="outproj_even",
    )(*streams, a, z, z, z, gb, conv_w, w_out, mods, ln_g, ln_b)


def _ffn_kernel(*refs, alpha, n_cast):
    casts_in, refs = refs[:n_cast], refs[n_cast:]
    x_ref, mod_ref, wg_ref, wu_ref, wd_ref, g_ref, b_ref, o_ref = refs[:8]
    casts_out, (h_ref, acc_ref) = refs[8:8 + n_cast], refs[8 + n_cast:]
    for src_ref, dst_ref in zip(casts_in, casts_out):
        dst_ref[...] = src_ref[...].astype(BF16)
    k = pl.program_id(1)

    @pl.when(k == 0)
    def _():
        h_ref[...] = (x_ref[...] * (1.0 + mod_ref[0, 4:5, :]) + mod_ref[0, 3:4, :]).astype(BF16)
        acc_ref[...] = jnp.zeros_like(acc_ref)

    h = h_ref[...]
    gate = jnp.dot(h, wg_ref[...], preferred_element_type=F32)
    up = jnp.dot(h, wu_ref[...], preferred_element_type=F32)
    act = (_silu(gate) * up).astype(BF16)
    acc_ref[...] += jnp.dot(act, wd_ref[...], preferred_element_type=F32)

    @pl.when(k == pl.num_programs(1) - 1)
    def _():
        o_ref[...] = _residual_ln(x_ref[...], acc_ref[...], mod_ref[0, 5:6, :], g_ref[...], b_ref[...], alpha)


def _ffn(xs, mods, wg, wu, wd, ln_g, ln_b, dims, rows, alpha, layer, j, tf, experts=None):
    _, d, dff = wg.shape
    nk = dff // tf
    tm = _pick_tile(512, dims.n, dims.r_ctx)
    n_row = rows // tm
    in_specs = [pl.BlockSpec((tm, d), lambda i, k: (i, 0)), _mod_spec(dims, tm, layer),
                pl.BlockSpec((None, d, tf), lambda i, k: (j, 0, k)),
                pl.BlockSpec((None, d, tf), lambda i, k: (j, 0, k)),
                pl.BlockSpec((None, tf, d), lambda i, k: (j, k, 0)),
                _layer_spec(ln_g, 2 * layer + 1), _layer_spec(ln_b, 2 * layer + 1)]
    out_specs = [pl.BlockSpec((tm, d), lambda i, k: (i, 0))]
    out_shape = [jax.ShapeDtypeStruct((rows, d), F32)]
    cast_in, cast_specs = [], []
    if experts is not None:
        (eg, eu, ed), jj = experts
        _, n_exp, _, dff = eg.shape
        assert dff == nk * tf
        rb = 16
        while n_row * rb < n_exp * d:
            rb *= 2
        nrb = n_exp * d // rb
        rbd = n_exp * tf // nrb
        assert n_exp * d % rb == 0 and n_exp * tf % nrb == 0 and rbd % 16 == 0
        gu_in = pl.BlockSpec((None, rb, tf), lambda i, k: (jj, jnp.minimum(i, nrb - 1), k))
        dn_in = pl.BlockSpec((None, rbd, d), lambda i, k: (jj, jnp.minimum(i, nrb - 1) * nk + k, 0))
        cast_in = [eg.reshape(-1, n_exp * d, dff), eu.reshape(-1, n_exp * d, dff), ed.reshape(-1, n_exp * dff, d)]
        cast_specs = [gu_in, gu_in, dn_in]
        gu_out = pl.BlockSpec((None, rb, tf), lambda i, k: (k, i, 0))
        out_specs += [gu_out, gu_out, pl.BlockSpec((rbd, d), lambda i, k: (i * nk + k, 0))]
        out_shape += [jax.ShapeDtypeStruct((nk, n_row * rb, tf), BF16)] * 2
        out_shape += [jax.ShapeDtypeStruct((n_row * nk * rbd, d), BF16)]
    res = pl.pallas_call(
        functools.partial(_ffn_kernel, alpha=alpha, n_cast=len(cast_in)),
        grid=(n_row, nk),
        in_specs=cast_specs + in_specs,
        out_specs=out_specs,
        out_shape=out_shape,
        scratch_shapes=[pltpu.VMEM((tm, d), BF16), pltpu.VMEM((tm, d), F32)],
        compiler_params=_cparams("arbitrary", "arbitrary"),
        name="dense_ffn",
    )(*cast_in, xs, mods, wg, wu, wd, ln_g, ln_b)
    return res[0], (tuple(res[1:]) if experts is not None else None)


def _inproj_odd_kernel(x_ref, mod_ref, w_ref, up_ref, uf_ref):
    h = (x_ref[...] * (1.0 + mod_ref[0, 1:2, :]) + mod_ref[0, 0:1, :]).astype(BF16)
    pw = up_ref.shape[1]
    up_ref[...] = jnp.dot(h, w_ref[:, 0:pw], preferred_element_type=F32).astype(BF16)
    uf_ref[...] = jnp.dot(h, w_ref[:, pw:], preferred_element_type=F32).astype(BF16)


def _inproj_odd(xs, mods, w_in, dims, rows, pool_width, layer, j):
    d = dims.d
    four_width = w_in.shape[2] - pool_width
    tm = _pick_tile(512, dims.n, dims.r_ctx)
    return pl.pallas_call(
        _inproj_odd_kernel,
        grid=(rows // tm,),
        in_specs=[_row_spec(tm, d), _mod_spec(dims, tm, layer), _layer_spec(w_in, j)],
        out_specs=[_row_spec(tm, pool_width), _row_spec(tm, four_width)],
        out_shape=[jax.ShapeDtypeStruct((rows, pool_width), BF16), jax.ShapeDtypeStruct((rows, four_width), BF16)],
        compiler_params=_cparams("parallel"),
        name="inproj_odd",
    )(xs, mods, w_in)


def _seq_dft_constants(n, group):
    n2 = n // FFT_RADIX
    b = np.arange(n2, dtype=np.float64)
    ka = np.arange(FFT_RADIX, dtype=np.float64)
    ang = 2.0 * np.pi * np.outer(b, ka) / n
    tw = np.concatenate([np.cos(ang), -np.sin(ang)], axis=1)
    ang2 = 2.0 * np.pi * np.outer(b, b) / n2
    cs, sn = np.cos(ang2), np.sin(ang2)
    mat = np.block([[cs, sn], [-sn, cs]]) / np.sqrt(float(n) * group)
    return jnp.asarray(tw, F32), jnp.asarray(mat, BF16)


def _seq_dft_kernel(x_ref, tw_ref, m_ref, or_ref, oi_ref):
    n2 = x_ref.shape[1]
    lanes = x_ref.shape[2]
    for ka in range(FFT_RADIX):
        yr = yi = None
        for a in range(FFT_RADIX):
            ang = 2.0 * np.pi * ((a * ka) % FFT_RADIX) / FFT_RADIX
            cr, ci = float(np.round(np.cos(ang), 12)), float(np.round(-np.sin(ang), 12))
            xa = x_ref[a].astype(F32)
            if cr != 0.0:
                yr = cr * xa if yr is None else yr + cr * xa
            if ci != 0.0:
                yi = ci * xa if yi is None else yi + ci * xa
        if ka == 0:
            zr, zi = yr, jnp.zeros_like(yr)
        else:
            twr = jnp.broadcast_to(tw_ref[:, ka:ka + 1], (n2, lanes))
            twi = jnp.broadcast_to(tw_ref[:, FFT_RADIX + ka:FFT_RADIX + ka + 1], (n2, lanes))
            if yi is None:
                zr, zi = yr * twr, yr * twi
            else:
                zr, zi = yr * twr - yi * twi, yr * twi + yi * twr
        zcat = jnp.concatenate([zr, zi], axis=0).astype(BF16)
        res = jnp.dot(m_ref[...], zcat, preferred_element_type=F32)
        or_ref[ka] = res[0:n2].astype(BF16)
        oi_ref[ka] = res[n2:].astype(BF16)


def _seq_dft(uf, row0, bsz, n, group):
    rows, width = uf.shape
    n2 = n // FFT_RADIX
    lanes = _pick_tile(FFT_LANES, width)
    tw, mat = _seq_dft_constants(n, group)
    x3 = uf.reshape(rows // n2, n2, width)
    slab0 = row0 // n
    out = jax.ShapeDtypeStruct((bsz, FFT_RADIX, n2, width), BF16)
    ospec = pl.BlockSpec((None, FFT_RADIX, n2, lanes), lambda b, c: (b, 0, 0, c))
    return pl.pallas_call(
        _seq_dft_kernel,
        grid=(bsz, width // lanes),
        in_specs=[pl.BlockSpec((FFT_RADIX, n2, lanes), lambda b, c: (slab0 + b, 0, c)),
                  pl.BlockSpec(tw.shape, lambda b, c: (0, 0)),
                  pl.BlockSpec(mat.shape, lambda b, c: (0, 0), pipeline_mode=pl.Buffered(1))],
        out_specs=[ospec, ospec],
        out_shape=[out, out],
        compiler_params=_cparams("parallel", "parallel"),
        name="seq_dft",
    )(x3, tw, mat)


def _top2_routing(logits, n_exp):
    lane = lax.broadcasted_iota(I32, logits.shape, 1)
    lg = jnp.where(lane < n_exp, logits, -jnp.inf)
    m1 = jnp.max(lg, axis=-1, keepdims=True)
    i1 = jnp.min(jnp.where(lg == m1, lane, ROUTE_LANES), axis=-1, keepdims=True)
    lg2 = jnp.where(lane == i1, -jnp.inf, lg)
    m2 = jnp.max(lg2, axis=-1, keepdims=True)
    i2 = jnp.min(jnp.where(lg2 == m2, lane, ROUTE_LANES), axis=-1, keepdims=True)
    e2 = jnp.exp(m2 - m1)
    g1 = 1.0 / (1.0 + e2)
    g2 = e2 * g1
    return jnp.where(lane == 0, i1.astype(F32),
                     jnp.where(lane == 1, i2.astype(F32), jnp.where(lane == 2, g1, jnp.where(lane == 3, g2, 0.0))))


def _outproj_odd_kernel(*refs, dims, alpha, has_ctx, n_exp):
    (up_ref, upp_ref, upn_ref, xr_ref, xi_ref) = refs[:5]
    refs = refs[5:]
    if has_ctx:
        (cr_ref, ci_ref) = refs[:2]
        refs = refs[2:]
    (perm_ref, cs_ref, fw_ref, pw_ref, ps_ref, w_ref, x_ref, mod_ref, g_ref, b_ref, rwh_ref, rwl_ref,
     o_ref, h2_ref, route_ref, ext_ref, cat_ref) = refs
    tm = x_ref.shape[0]
    i = pl.program_id(0)
    row = lax.broadcasted_iota(I32, (tm, 1), 0)
    pos, length = _seq_position(i * tm + row, dims)
    first_pos, _ = _seq_position(i * tm, dims)
    last_pos, last_len = _seq_position(i * tm + tm - 1, dims)

    pool_width = up_ref.shape[1]
    group = pool_width // len(POOL_WINDOWS)
    h0 = POOL_HALO
    ext_ref[0:h0, :] = jnp.where(first_pos == 0, 0.0, upp_ref[HALO_ROWS - h0:HALO_ROWS, :].astype(F32))
    ext_ref[h0:h0 + tm, :] = up_ref[...].astype(F32)
    ext_ref[h0 + tm:h0 + tm + h0, :] = jnp.where(last_pos == last_len - 1, 0.0, upn_ref[0:h0, :].astype(F32))
    for gi, w in enumerate(POOL_WINDOWS):
        cols = slice(gi * group, (gi + 1) * group)
        back, ahead = w // 2, w - w // 2 - 1
        tot = None
        for s in range(-back, ahead + 1):
            t = ext_ref[h0 + s:h0 + s + tm, cols]
            tot = t if tot is None else tot + t
        cnt = jnp.minimum(pos + ahead, length - 1) - jnp.maximum(pos - back, 0) + 1
        pooled = tot / cnt.astype(F32) - ext_ref[h0:h0 + tm, cols]
        yp = jnp.dot(pooled.astype(BF16), pw_ref[gi], preferred_element_type=F32) * ps_ref[:, cols]
        cat_ref[:, cols] = yp.astype(BF16)

    if has_ctx:
        lat = i * tm < dims.r_lat
        xr = jnp.where(lat, xr_ref[...], cr_ref[...])
        xi = jnp.where(lat, xi_ref[...], ci_ref[...])
    else:
        xr, xi = xr_ref[...], xi_ref[...]
    four_width = xr.shape[-1]
    fgroup = four_width // FOURIER_HEADS
    xr = jnp.dot(perm_ref[...], xr.reshape(tm, four_width), preferred_element_type=F32).astype(BF16)
    xi = jnp.dot(perm_ref[...], xi.reshape(tm, four_width), preferred_element_type=F32).astype(BF16)
    for hd in range(FOURIER_HEADS):
        cols = slice(hd * fgroup, (hd + 1) * fgroup)
        both = jnp.concatenate([xr[:, cols], xi[:, cols]], axis=1)
        f = jnp.dot(both, cs_ref[...], preferred_element_type=F32)
        yf = jnp.dot(f.astype(BF16), fw_ref[hd], preferred_element_type=F32)
        cat_ref[:, pool_width + hd * fgroup:pool_width + (hd + 1) * fgroup] = yf.astype(BF16)

    mix = jnp.dot(cat_ref[...], w_ref[...], preferred_element_type=F32)
    xn = _residual_ln(x_ref[...], mix, mod_ref[0, 2:3, :], g_ref[...], b_ref[...], alpha)
    o_ref[...] = xn
    h2 = xn * (1.0 + mod_ref[0, 4:5, :]) + mod_ref[0, 3:4, :]
    h2_ref[...] = h2
    h_hi = h2.astype(BF16)
    h_lo = (h2 - h_hi.astype(F32)).astype(BF16)
    logits = (jnp.dot(h_hi, rwh_ref[...], preferred_element_type=F32)
              + (jnp.dot(h_lo, rwh_ref[...], preferred_element_type=F32)
                 + jnp.dot(h_hi, rwl_ref[...], preferred_element_type=F32)))
    route_ref[...] = _top2_routing(logits, n_exp)


def _channel_dft_matrix(group):
    c = np.arange(group, dtype=np.float64)
    ang = 2.0 * np.pi * np.outer(c, c) / group
    return jnp.asarray(np.concatenate([np.cos(ang), np.sin(ang)], axis=0), BF16)


def _tile_permutation(tm):
    per = tm // FFT_RADIX
    p = np.zeros((tm, tm), np.float32)
    for ka in range(FFT_RADIX):
        for kb in range(per):
            p[kb * FFT_RADIX + ka, ka * per + kb] = 1.0
    return jnp.asarray(p, BF16)


def _outproj_odd(up, lat_ri, ctx_ri, fourier_w, pool_w, pool_scale, w_out, xs, mods, ln_g, ln_b, rw_hi, rw_lo,
                 n_exp, dims, rows, alpha, layer, j):
    d = dims.d
    tm = SEQ_TILE
    pool_width = up.shape[1]
    four_width = lat_ri[0].shape[-1]
    fgroup = four_width // FOURIER_HEADS
    has_ctx = ctx_ri is not None
    per = tm // FFT_RADIX
    lat_tiles = dims.n // tm
    n_lat = dims.r_lat // tm
    upp_spec, upn_spec = _halo_specs(tm, pool_width, up.shape[0])

    def lat_map(i):
        t = jnp.minimum(i, n_lat - 1)
        return (t // lat_tiles, 0, t % lat_tiles, 0)

    lat_spec = pl.BlockSpec((None, FFT_RADIX, per, four_width), lat_map)
    operands = [up, up, up, lat_ri[0], lat_ri[1]]
    specs = [_row_spec(tm, pool_width), upp_spec, upn_spec, lat_spec, lat_spec]
    if has_ctx:
        ctx_tiles = dims.n_ctx // tm

        def ctx_map(i):
            t = jnp.maximum(i - n_lat, 0)
            return (t // ctx_tiles, 0, t % ctx_tiles, 0)

        ctx_spec = pl.BlockSpec((None, FFT_RADIX, per, four_width), ctx_map)
        operands += [ctx_ri[0], ctx_ri[1]]
        specs += [ctx_spec, ctx_spec]
    perm, cs = _tile_permutation(tm), _channel_dft_matrix(fgroup)
    operands += [perm, cs, fourier_w, pool_w, pool_scale, w_out, xs, mods, ln_g, ln_b, rw_hi, rw_lo]
    specs += [_const_spec(perm.shape), _const_spec(cs.shape), _layer_spec(fourier_w, j), _layer_spec(pool_w, j),
              _layer_spec(pool_scale, j), _layer_spec(w_out, layer), _row_spec(tm, d),
              _mod_spec(dims, tm, layer), _layer_spec(ln_g, 2 * layer), _layer_spec(ln_b, 2 * layer),
              _layer_spec(rw_hi, j), _layer_spec(rw_lo, j)]
    return pl.pallas_call(
        functools.partial(_outproj_odd_kernel, dims=dims, alpha=alpha, has_ctx=has_ctx, n_exp=n_exp),
        grid=(rows // tm,),
        in_specs=specs,
        out_specs=[_row_spec(tm, d), _row_spec(tm, d), _row_spec(tm, ROUTE_LANES)],
        out_shape=[jax.ShapeDtypeStruct((rows, d), F32), jax.ShapeDtypeStruct((rows, d), F32),
                   jax.ShapeDtypeStruct((rows, ROUTE_LANES), F32)],
        scratch_shapes=[pltpu.VMEM((tm + 2 * POOL_HALO, pool_width), F32), pltpu.VMEM((tm, d), BF16)],
        compiler_params=_cparams("parallel"),
        name="outproj_odd",
    )(*operands)


def _dispatch_plan(route, n_exp, tile):
    tokens = route.shape[0]
    assign = TOP_K * tokens
    experts = route[:, 0:TOP_K].astype(I32).reshape(assign)
    onehot = (experts[:, None] == jnp.arange(n_exp, dtype=I32)[None, :]).astype(I32)
    running = jnp.cumsum(onehot, axis=0)
    rank = jnp.sum((running - onehot) * onehot, axis=1)
    counts = running[-1]
    padded = ((counts + tile - 1) // tile) * tile
    ends = jnp.cumsum(padded)
    starts = ends - padded
    slot = jnp.sum(onehot * starts[None, :], axis=1) + rank
    n_tiles = -(-(assign + n_exp * (tile - 1)) // tile)
    tile_start = jnp.arange(n_tiles, dtype=I32) * tile
    tile_expert = jnp.minimum(jnp.sum((tile_start[:, None] >= ends[None, :]).astype(I32), axis=1), n_exp - 1)
    live_tiles = ends[-1] // tile
    tile_expert = jnp.where(tile_start < ends[-1], tile_expert, tile_expert[jnp.maximum(live_tiles - 1, 0)])
    pad_first = starts + counts
    pad_head = jnp.minimum((-pad_first) % SUBLANES, padded - counts)
    pad_info = jnp.concatenate([pad_first, pad_head, pad_first + pad_head, padded - counts - pad_head,
                                live_tiles[None]]).astype(I32)
    return slot, pad_info, tile_expert, n_tiles


def _dispatch_kernel(slot_ref, pad_ref, h_ref, o_ref, zero_ref, sem, zsem, *, n_exp, tile, n_tiles, min_tiles):
    i = pl.program_id(0)
    tm = h_ref.shape[0]

    def zero_copy(first_row, size):
        return pltpu.make_async_copy(zero_ref.at[pl.ds(0, size)], o_ref.at[pl.ds(first_row, size)], zsem)

    def padding_copies(act):
        for e in range(n_exp):
            first, head = pad_ref[e], pad_ref[n_exp + e]
            for r in range(SUBLANES - 1):
                pl.when(r < head)(functools.partial(act, zero_copy(first + r, 1)))
            first, length = pad_ref[2 * n_exp + e], pad_ref[3 * n_exp + e]
            for bit in reversed(range(SUBLANES.bit_length() - 1, tile.bit_length() - 1)):
                size = 1 << bit
                for part in range(-(-size // ZERO_ROWS)):
                    rows = min(size, ZERO_ROWS)
                    at = pl.multiple_of(first + part * rows, SUBLANES)
                    pl.when((length & size) != 0)(functools.partial(act, zero_copy(at, rows)))
                first = first + (length & size)
        live = pad_ref[PAD_LIVE * n_exp]
        for t in range(min_tiles, n_tiles):
            for part in range(tile // ZERO_ROWS):
                pl.when(t >= live)(functools.partial(act, zero_copy(t * tile + part * ZERO_ROWS, ZERO_ROWS)))

    @pl.when(i == 0)
    def _():
        zero_ref[...] = jnp.zeros_like(zero_ref)
        padding_copies(lambda cp: cp.start())
        padding_copies(lambda cp: cp.wait())

    base = i * tm * TOP_K

    def issue(r, carry):
        for c in range(TOP_K):
            dst = slot_ref[base + r * TOP_K + c]
            pltpu.make_async_copy(h_ref.at[pl.ds(r, 1)], o_ref.at[pl.ds(dst, 1)], sem).start()
        return carry

    lax.fori_loop(0, tm, issue, 0, unroll=ISSUE_UNROLL)
    for c in range(TOP_K):
        pltpu.make_async_copy(h_ref, o_ref.at[pl.ds(0, tm)], sem).wait()


def _dispatch_rows(slot, pad_info, h2, n_exp, tile, n_tiles):
    tokens, d = h2.shape
    tm = SEQ_TILE
    min_tiles = -(-(TOP_K * tokens) // tile)
    return pl.pallas_call(
        functools.partial(_dispatch_kernel, n_exp=n_exp, tile=tile, n_tiles=n_tiles, min_tiles=min_tiles),
        grid_spec=pltpu.PrefetchScalarGridSpec(
            num_scalar_prefetch=2,
            grid=(tokens // tm,),
            in_specs=[_row_spec(tm, d)],
            out_specs=pl.BlockSpec(memory_space=pl.ANY),
            scratch_shapes=[pltpu.VMEM((ZERO_ROWS, d), h2.dtype), pltpu.SemaphoreType.DMA(()),
                            pltpu.SemaphoreType.DMA(())]),
        out_shape=jax.ShapeDtypeStruct((n_tiles * tile, d), h2.dtype),
        compiler_params=_cparams("arbitrary", unchecked=True),
        name="moe_dispatch",
    )(slot, pad_info, h2)


def _expert_kernel(te_ref, pad_ref, x_ref, wg_ref, wu_ref, wd_ref, o_ref, wgb_ref, wub_ref, wdb_ref, h_ref, acc_ref,
                   sem, *, n_exp, nk):
    i = pl.program_id(0)
    live_tiles = pad_ref[PAD_LIVE * n_exp]
    d = x_ref.shape[1]
    tf = wgb_ref.shape[2]

    def chunk_copies(expert, c, slot):
        rows = pl.ds(pl.multiple_of(expert * d, d), d)
        down = pl.ds(pl.multiple_of((expert * nk + c) * tf, tf), tf)
        return (pltpu.make_async_copy(wg_ref.at[c, rows, :], wgb_ref.at[slot], sem.at[0, slot]),
                pltpu.make_async_copy(wu_ref.at[c, rows, :], wub_ref.at[slot], sem.at[1, slot]),
                pltpu.make_async_copy(wd_ref.at[down, :], wdb_ref.at[slot], sem.at[2, slot]))

    @pl.when(i < live_tiles)
    def _():
        expert = te_ref[i]

        @pl.when(i == 0)
        def _():
            for cp in chunk_copies(expert, 0, 0):
                cp.start()

        h_ref[...] = x_ref[...].astype(BF16)
        for c in range(nk):
            slot = c % WEIGHT_SLOTS
            for cp in chunk_copies(expert, c, slot):
                cp.wait()
            if c + 1 < nk:
                for cp in chunk_copies(expert, c + 1, (c + 1) % WEIGHT_SLOTS):
                    cp.start()
            else:
                @pl.when(i + 1 < live_tiles)
                def _():
                    for cp in chunk_copies(te_ref[i + 1], 0, 0):
                        cp.start()
            h = h_ref[...]
            gate = jnp.dot(h, wgb_ref[slot], preferred_element_type=F32)
            up = jnp.dot(h, wub_ref[slot], preferred_element_type=F32)
            act = (_silu(gate) * up).astype(BF16)
            part = jnp.dot(act, wdb_ref[slot], preferred_element_type=F32)
            if c == 0:
                acc_ref[...] = part
            else:
                acc_ref[...] += part
        o_ref[...] = acc_ref[...]

    @pl.when(i >= live_tiles)
    def _():
        o_ref[...] = jnp.zeros_like(o_ref)


def _expert_ffn(tile_expert, pad_info, xs, wg, wu, wd, n_exp):
    n_slots, d = xs.shape
    nk, _, tf = wg.shape
    tm = EXPERT_TILE
    assert (nk - 1) % WEIGHT_SLOTS != 0
    return pl.pallas_call(
        functools.partial(_expert_kernel, n_exp=n_exp, nk=nk),
        grid_spec=pltpu.PrefetchScalarGridSpec(
            num_scalar_prefetch=2,
            grid=(n_slots // tm,),
            in_specs=[pl.BlockSpec((tm, d), lambda i, te, pad: (i, 0)),
                      pl.BlockSpec(memory_space=pl.ANY), pl.BlockSpec(memory_space=pl.ANY),
                      pl.BlockSpec(memory_space=pl.ANY)],
            out_specs=pl.BlockSpec((tm, d), lambda i, te, pad: (i, 0)),
            scratch_shapes=[pltpu.VMEM((WEIGHT_SLOTS, d, tf), BF16), pltpu.VMEM((WEIGHT_SLOTS, d, tf), BF16),
                            pltpu.VMEM((WEIGHT_SLOTS, tf, d), BF16), pltpu.VMEM((tm, d), BF16),
                            pltpu.VMEM((tm, d), F32), pltpu.SemaphoreType.DMA((3, WEIGHT_SLOTS))]),
        out_shape=jax.ShapeDtypeStruct((n_slots, d), F32),
        compiler_params=_cparams("arbitrary"),
        name="expert_ffn",
    )(tile_expert, pad_info, xs, wg, wu, wd)


def _combine_kernel(slot_ref, y_ref, route_ref, x_ref, mod_ref, g_ref, b_ref, o_ref, buf_ref, sem, *, alpha):
    tm = x_ref.shape[0]
    i = pl.program_id(0)

    def fetch(tile, half):
        base = tile * tm * TOP_K

        def issue(r, carry):
            for c in range(TOP_K):
                src = slot_ref[base + r * TOP_K + c]
                pltpu.make_async_copy(y_ref.at[pl.ds(src, 1)], buf_ref.at[half, c, pl.ds(r, 1)],
                                      sem.at[half]).start()
            return carry

        lax.fori_loop(0, tm, issue, 0, unroll=ISSUE_UNROLL)

    @pl.when(i == 0)
    def _():
        fetch(0, 0)

    @pl.when(i + 1 < pl.num_programs(0))
    def _():
        fetch(i + 1, (i + 1) % 2)

    half = i % 2
    for c in range(TOP_K):
        pltpu.make_async_copy(y_ref.at[pl.ds(0, tm)], buf_ref.at[half, c], sem.at[half]).wait()
    route = route_ref[...]
    ff = route[:, 2:3] * buf_ref[half, 0] + route[:, 3:4] * buf_ref[half, 1]
    o_ref[...] = _residual_ln(x_ref[...], ff, mod_ref[0, 5:6, :], g_ref[...], b_ref[...], alpha)


def _combine(slot, ys, route, xs, mods, ln_g, ln_b, dims, rows, alpha, layer):
    d = dims.d
    tm = SEQ_TILE
    return pl.pallas_call(
        functools.partial(_combine_kernel, alpha=alpha),
        grid_spec=pltpu.PrefetchScalarGridSpec(
            num_scalar_prefetch=1,
            grid=(rows // tm,),
            in_specs=[pl.BlockSpec(memory_space=pl.ANY), _row_spec(tm, ROUTE_LANES), _row_spec(tm, d),
                      _mod_spec(dims, tm, layer), _layer_spec(ln_g, 2 * layer + 1),
                      _layer_spec(ln_b, 2 * layer + 1)],
            out_specs=_row_spec(tm, d),
            scratch_shapes=[pltpu.VMEM((2, TOP_K, tm, d), F32), pltpu.SemaphoreType.DMA((2,))]),
        out_shape=jax.ShapeDtypeStruct((rows, d), F32),
        compiler_params=_cparams("arbitrary", unchecked=True),
        name="moe_combine",
    )(slot, ys, route, xs, mods, ln_g, ln_b)


def _moe(xs, h2, route, mods, wg, wu, wd, n_exp, ln_g, ln_b, dims, rows, alpha, layer):
    slot, pad_info, tile_expert, n_tiles = _dispatch_plan(route, n_exp, EXPERT_TILE)
    sorted_rows = _dispatch_rows(slot, pad_info, h2, n_exp, EXPERT_TILE, n_tiles)
    ys = _expert_ffn(tile_expert, pad_info, sorted_rows, wg, wu, wd, n_exp)
    return _combine(slot, ys, route, xs, mods, ln_g, ln_b, dims, rows, alpha, layer)


def kernel(x, c, ctx, c_ctx, w_mod, b_mod, w_mix_out, ln_g, ln_b, w_in_ab, conv_w, attn_sink, w_in_cd, pool_w,
           pool_scale, fourier_w, ffn_w_gate, ffn_w_up, ffn_w_down, router_w, moe_w_gate, moe_w_up, moe_w_down):
    bsz, n, d = x.shape
    n_ctx = ctx.shape[1]
    depth = w_mod.shape[0]
    n_exp = router_w.shape[2]
    dims = Dims(bsz, n, n_ctx, d)
    assert n % SEQ_TILE == 0 and n_ctx % SEQ_TILE == 0 and n % GRID_W == 0
    assert dims.r_lat % (FFT_RADIX * (n_ctx // FFT_RADIX)) == 0 and dims.r_all % (n // FFT_RADIX) == 0
    alpha = (2.0 * depth) ** 0.25
    pool_width = pool_w.shape[1] * pool_w.shape[2]
    fgroup = fourier_w.shape[2]

    mods = _modulations(c, c_ctx, w_mod, b_mod)
    w_out = w_mix_out.astype(BF16)
    w_ab, w_cd = w_in_ab.astype(BF16), w_in_cd.astype(BF16)
    tf = _pick_tile(512, ffn_w_gate.shape[2], moe_w_gate.shape[3])
    ffn_g, ffn_u, ffn_d = ffn_w_gate.astype(BF16), ffn_w_up.astype(BF16), ffn_w_down.astype(BF16)
    four_w, pool_wb = fourier_w.astype(BF16), pool_w.astype(BF16)
    pool_sc = pool_scale[:, None, :]
    lng, lnb = ln_g.reshape(2 * depth, 1, d), ln_b.reshape(2 * depth, 1, d)
    rw = jnp.zeros(router_w.shape[:2] + (ROUTE_LANES,), F32).at[:, :, :n_exp].set(router_w)
    rw_hi = rw.astype(BF16)
    rw_lo = (rw - rw_hi.astype(F32)).astype(BF16)

    streams = [x.reshape(dims.r_lat, d), ctx.reshape(dims.r_ctx, d)]
    expert_bf = None
    for l in range(depth):
        even = l % 2 == 0
        j = l // 2
        ctx_out = any(m % 2 == 0 for m in range(l + 1, depth))
        rows = dims.r_all if ctx_out else dims.r_lat
        if even:
            q, k, v, z, gb = _inproj_even(streams, mods, w_ab, dims, l, j)
            a = _attention(q, k, v, attn_sink, dims, ctx_out, j)
            xs = _outproj_even(a, z, gb, conv_w, w_out, streams, mods, lng, lnb, dims, rows, alpha, l, j)
            experts = ((moe_w_gate, moe_w_up, moe_w_down), j) if l + 1 < depth else None
            xs, expert_bf = _ffn(xs, mods, ffn_g, ffn_u, ffn_d, lng, lnb, dims, rows, alpha, l, j, tf, experts)
        else:
            xs = streams[0]
            up, uf = _inproj_odd(xs, mods, w_cd, dims, rows, pool_width, l, j)
            lat_ri = _seq_dft(uf, 0, bsz, n, fgroup)
            ctx_ri = _seq_dft(uf, dims.r_lat, bsz, n_ctx, fgroup) if ctx_out else None
            xs, h2, route = _outproj_odd(up, lat_ri, ctx_ri, four_w, pool_wb, pool_sc, w_out, xs, mods, lng, lnb,
                                         rw_hi, rw_lo, n_exp, dims, rows, alpha, l, j)
            xs = _moe(xs, h2, route, mods, *expert_bf, n_exp, lng, lnb, dims, rows, alpha, l)
        streams = [xs]
    return xs[:dims.r_lat].reshape(bsz, n, d)
```

```python
import functools
from typing import NamedTuple

import numpy as np
import jax
import jax.numpy as jnp
from jax import lax
from jax.experimental import pallas as pl
from jax.experimental.pallas import tpu as pltpu

F32 = jnp.float32
BF16 = jnp.bfloat16
I32 = jnp.int32

HEAD_DIM = 128
N_Q_HEADS = 8
N_KV_HEADS = 2
Q_PER_KV = N_Q_HEADS // N_KV_HEADS
ATTN_WIDTH = N_Q_HEADS * HEAD_DIM
KV_WIDTH = N_KV_HEADS * HEAD_DIM
ATTN_BLOCK = 128
GRID_W = 64
ROPE_THETA = 10000.0
NEG_INF = -1e30
POOL_WINDOWS = (2, 4, 8, 16)
POOL_HALO = 8
HALO_ROWS = 16
FOURIER_HEADS = 4
N_MOD = 6
LN_EPS = 1e-5
TOP_K = 2
FFT_RADIX = 8
FFT_LANES = 256
SEQ_TILE = 256
ROUTE_LANES = 128
EXPERT_TILE = 512
WEIGHT_SLOTS = 3
ISSUE_UNROLL = 8
ZERO_ROWS = 256
SUBLANES = 8
PAD_LIVE = 4
V7X_VMEM_LIMIT = 56 * 1024 * 1024


class Dims(NamedTuple):
    bsz: int
    n: int
    n_ctx: int
    d: int

    @property
    def r_lat(self):
        return self.bsz * self.n

    @property
    def r_ctx(self):
        return self.bsz * self.n_ctx

    @property
    def r_all(self):
        return self.r_lat + self.r_ctx


def _pick_tile(pref, *dims):
    t = pref
    while any(d % t for d in dims):
        t //= 2
    return t


def _cparams(*sem, unchecked=False):
    return pltpu.CompilerParams(dimension_semantics=sem, vmem_limit_bytes=V7X_VMEM_LIMIT,
                                disable_bounds_checks=unchecked)


def _residual_ln(x, y, gate, g, b, alpha):
    v = alpha * x + gate * y
    mu = jnp.mean(v, axis=-1, keepdims=True)
    vc = v - mu
    var = jnp.mean(vc * vc, axis=-1, keepdims=True)
    return vc * lax.rsqrt(var + LN_EPS) * g + b


def _silu(t):
    return t * jax.nn.sigmoid(t)


def _seq_position(row, dims):
    lat = row < dims.r_lat
    pos = jnp.where(lat, lax.rem(row, dims.n), lax.rem(row - dims.r_lat, dims.n_ctx))
    length = jnp.where(lat, dims.n, dims.n_ctx)
    return pos, length


def _mod_spec(dims, tm, layer):
    return pl.BlockSpec((None, 1, N_MOD, dims.d), lambda i, *_: (layer, (i * tm) // dims.n, 0, 0))


def _row_spec(tm, width):
    return pl.BlockSpec((tm, width), lambda i, *_: (i, 0))


def _stream_specs(streams, tm, dims):
    if len(streams) == 1:
        return [_row_spec(tm, dims.d)]
    n_lat = dims.r_lat // tm
    return [pl.BlockSpec((tm, dims.d), lambda i, *_: (jnp.minimum(i, n_lat - 1), 0)),
            pl.BlockSpec((tm, dims.d), lambda i, *_: (jnp.maximum(i - n_lat, 0), 0))]


def _stream_rows(refs, dims):
    if len(refs) == 1:
        return refs[0][...]
    tm = refs[0].shape[0]
    return jnp.where(pl.program_id(0) * tm < dims.r_lat, refs[0][...], refs[1][...])


def _layer_spec(arr, index, **kw):
    zeros = (0,) * (arr.ndim - 1)
    return pl.BlockSpec((None,) + arr.shape[1:], lambda *_: (index,) + zeros, **kw)


def _const_spec(shape):
    zeros = (0,) * len(shape)
    return pl.BlockSpec(shape, lambda *_: zeros)


def _mod_kernel(cond_ref, w_ref, b_ref, o_ref):
    s = _silu(cond_ref[...]).astype(BF16)
    o_ref[...] = jnp.dot(s, w_ref[...].astype(BF16), preferred_element_type=F32) + b_ref[...]


def _modulations(c, c_ctx, w_mod, b_mod):
    depth, d, nmod = w_mod.shape
    bsz = c.shape[0]
    rows = -(-(bsz + 1) // 8) * 8
    cond = jnp.zeros((rows, d), F32).at[:bsz].set(c).at[bsz].set(c_ctx)
    tn = _pick_tile(1024, nmod)
    out = pl.pallas_call(
        _mod_kernel,
        grid=(depth, nmod // tn),
        in_specs=[pl.BlockSpec((rows, d), lambda l, j: (0, 0)),
                  pl.BlockSpec((None, d, tn), lambda l, j: (l, 0, j)),
                  pl.BlockSpec((None, 1, tn), lambda l, j: (l, 0, j))],
        out_specs=pl.BlockSpec((None, rows, tn), lambda l, j: (l, 0, j)),
        out_shape=jax.ShapeDtypeStruct((depth, rows, nmod), F32),
        compiler_params=_cparams("arbitrary", "arbitrary"),
        name="modulation",
    )(cond, w_mod, b_mod.reshape(depth, 1, nmod))
    return out.reshape(depth, rows, N_MOD, d)


def _rope_tables(n, tm):
    rows = n // GRID_W
    row = jnp.repeat(jnp.arange(rows, dtype=F32), GRID_W)
    col = jnp.tile(jnp.arange(GRID_W, dtype=F32), rows)
    half = HEAD_DIM // 2
    inv = ROPE_THETA ** (-jnp.arange(0, half, 2, dtype=F32) / half)
    ang_r = row[:, None] * inv
    ang_c = col[:, None] * inv
    ang = jnp.concatenate([ang_r, ang_r, ang_c, ang_c], -1)
    cos, sin = jnp.cos(ang), jnp.sin(ang)
    quarter = HEAD_DIM // 4
    first = (jnp.arange(HEAD_DIM) // quarter) % 2 == 0
    sin_up = jnp.where(first, -sin, 0.0)
    sin_dn = jnp.where(first, 0.0, sin)
    ident = jnp.zeros((tm, HEAD_DIM), F32)
    return (jnp.concatenate([cos, ident + 1.0], 0), jnp.concatenate([sin_up, ident], 0),
            jnp.concatenate([sin_dn, ident], 0))


def _inproj_even_kernel(*refs, conv_width, dims, n_streams):
    x_refs, refs = refs[:n_streams], refs[n_streams:]
    mod_ref, w_ref, cos_ref, sup_ref, sdn_ref, q_ref, k_ref, v_ref, z_ref, gb_ref = refs
    h = (_stream_rows(x_refs, dims) * (1.0 + mod_ref[0, 1:2, :]) + mod_ref[0, 0:1, :]).astype(BF16)
    cos, sup, sdn = cos_ref[...], sup_ref[...], sdn_ref[...]
    quarter = HEAD_DIM // 4

    def rope(u):
        return u * cos + pltpu.roll(u, HEAD_DIM - quarter, 1) * sup + pltpu.roll(u, quarter, 1) * sdn

    q = jnp.dot(h, w_ref[:, 0:ATTN_WIDTH], preferred_element_type=F32)
    for hd in range(N_Q_HEADS):
        sl = slice(hd * HEAD_DIM, (hd + 1) * HEAD_DIM)
        q_ref[:, sl] = rope(q[:, sl]).astype(BF16)
    o = ATTN_WIDTH
    kv = jnp.dot(h, w_ref[:, o:o + 2 * KV_WIDTH], preferred_element_type=F32)
    for hd in range(N_KV_HEADS):
        sl = slice(hd * HEAD_DIM, (hd + 1) * HEAD_DIM)
        k_ref[:, sl] = rope(kv[:, sl]).astype(BF16)
    v_ref[...] = kv[:, KV_WIDTH:].astype(BF16)
    o += 2 * KV_WIDTH
    u = jnp.dot(h, w_ref[:, o:o + conv_width], preferred_element_type=F32)
    gc = jnp.dot(h, w_ref[:, o + 2 * conv_width:o + 3 * conv_width], preferred_element_type=F32)
    z_ref[...] = (gc * u).astype(BF16)
    gb = jnp.dot(h, w_ref[:, o + conv_width:o + 2 * conv_width], preferred_element_type=F32)
    gb_ref[...] = gb.astype(BF16)


def _inproj_even(streams, mods, w_in, dims, layer, j):
    rows = dims.r_all
    conv_width = (w_in.shape[2] - ATTN_WIDTH - 2 * KV_WIDTH) // 3
    tm = _pick_tile(512, dims.n, dims.r_ctx)
    cos, sup, sdn = _rope_tables(dims.n, tm)
    nlat = dims.n // tm
    tab_spec = pl.BlockSpec((tm, HEAD_DIM), lambda i: (jnp.where(i * tm < dims.r_lat, i % nlat, nlat), 0))
    return pl.pallas_call(
        functools.partial(_inproj_even_kernel, conv_width=conv_width, dims=dims, n_streams=len(streams)),
        grid=(rows // tm,),
        in_specs=_stream_specs(streams, tm, dims)
        + [_mod_spec(dims, tm, layer), _layer_spec(w_in, j, pipeline_mode=pl.Buffered(1)),
           tab_spec, tab_spec, tab_spec],
        out_specs=[_row_spec(tm, ATTN_WIDTH), _row_spec(tm, KV_WIDTH), _row_spec(tm, KV_WIDTH),
                   _row_spec(tm, conv_width), _row_spec(tm, conv_width)],
        out_shape=[jax.ShapeDtypeStruct((rows, ATTN_WIDTH), BF16), jax.ShapeDtypeStruct((rows, KV_WIDTH), BF16),
                   jax.ShapeDtypeStruct((rows, KV_WIDTH), BF16), jax.ShapeDtypeStruct((rows, conv_width), BF16),
                   jax.ShapeDtypeStruct((rows, conv_width), BF16)],
        compiler_params=_cparams("parallel"),
        name="inproj_even",
    )(*streams, mods, w_in, cos, sup, sdn)


def _attn_kernel(sink_ref, q_ref, kp_ref, kc_ref, kn_ref, kx_ref, vp_ref, vc_ref, vn_ref, vx_ref, o_ref, *,
                 nb, n_ctx, j):
    i = pl.program_id(1)
    blk = ATTN_BLOCK
    rows = Q_PER_KV * blk
    latent = i < nb
    has_prev = jnp.logical_and(latent, i > 0)
    has_next = jnp.logical_and(latent, i < nb - 1)
    qi = lax.broadcasted_iota(I32, (blk, blk), 0)
    kj = lax.broadcasted_iota(I32, (blk, blk), 1)
    bias_prev = jnp.where(jnp.logical_and(kj >= qi, has_prev), 0.0, NEG_INF).astype(F32)
    bias_next = jnp.where(jnp.logical_and(kj <= qi, has_next), 0.0, NEG_INF).astype(F32)
    bias_prev = jnp.concatenate([bias_prev] * Q_PER_KV, axis=0)
    bias_next = jnp.concatenate([bias_next] * Q_PER_KV, axis=0)
    bias_own = jnp.where(latent, 0.0, NEG_INF).astype(F32)
    grp = lax.broadcasted_iota(I32, (rows, 1), 0) // blk
    scale = HEAD_DIM ** -0.5
    for hk in range(N_KV_HEADS):
        sl = slice(hk * HEAD_DIM, (hk + 1) * HEAD_DIM)
        kcat = jnp.concatenate([kx_ref[:, sl], kp_ref[:, sl], kc_ref[:, sl], kn_ref[:, sl]], axis=0)
        vcat = jnp.concatenate([vx_ref[:, sl], vp_ref[:, sl], vc_ref[:, sl], vn_ref[:, sl]], axis=0)
        heads = [hk * Q_PER_KV + g for g in range(Q_PER_KV)]
        qs = jnp.concatenate([q_ref[:, hq * HEAD_DIM:(hq + 1) * HEAD_DIM] for hq in heads], axis=0)
        s = lax.dot_general(qs, kcat, (((1,), (1,)), ((), ())), preferred_element_type=F32) * scale
        s = jnp.concatenate([s[:, 0:n_ctx], s[:, n_ctx:n_ctx + blk] + bias_prev,
                             s[:, n_ctx + blk:n_ctx + 2 * blk] + bias_own, s[:, n_ctx + 2 * blk:] + bias_next], axis=1)
        sink = jnp.full((rows, 1), sink_ref[j, heads[-1]], F32)
        for g in range(Q_PER_KV - 1):
            sink = jnp.where(grp == g, sink_ref[j, heads[g]], sink)
        m = jnp.maximum(jnp.max(s, axis=-1, keepdims=True), sink)
        p = jnp.exp(s - m)
        den = jnp.sum(p, axis=-1, keepdims=True) + jnp.exp(sink - m)
        pn = (p * (1.0 / den)).astype(BF16)
        o = jnp.dot(pn, vcat, preferred_element_type=F32)
        for g, hq in enumerate(heads):
            o_ref[:, hq * HEAD_DIM:(hq + 1) * HEAD_DIM] = o[g * blk:(g + 1) * blk].astype(BF16)


def _attention(q, k, v, sink, dims, ctx_out, j):
    blk = ATTN_BLOCK
    nb = dims.n // blk
    nqc = dims.n_ctx // blk if ctx_out else 0
    rows_out = dims.r_all if ctx_out else dims.r_lat
    lat_blocks = dims.r_lat // blk

    def q_map(b, i, *_):
        return (jnp.where(i < nb, b * nb + i, lat_blocks + b * nqc + (i - nb)), 0)

    def k_map(shift):
        return lambda b, i, *_: (b * nb + jnp.clip(i + shift, 0, nb - 1), 0)

    def ctx_map(b, i, *_):
        return (dims.r_lat // dims.n_ctx + b, 0)

    kv_specs = [pl.BlockSpec((blk, KV_WIDTH), k_map(-1)), pl.BlockSpec((blk, KV_WIDTH), k_map(0)),
                pl.BlockSpec((blk, KV_WIDTH), k_map(1)), pl.BlockSpec((dims.n_ctx, KV_WIDTH), ctx_map)]
    return pl.pallas_call(
        functools.partial(_attn_kernel, nb=nb, n_ctx=dims.n_ctx, j=j),
        grid=(dims.bsz, nb + nqc),
        in_specs=[pl.BlockSpec(memory_space=pltpu.SMEM), pl.BlockSpec((blk, ATTN_WIDTH), q_map)]
        + kv_specs + kv_specs,
        out_specs=pl.BlockSpec((blk, ATTN_WIDTH), q_map),
        out_shape=jax.ShapeDtypeStruct((rows_out, ATTN_WIDTH), BF16),
        compiler_params=_cparams("parallel", "parallel"),
        name="band_attention",
    )(sink, q, k, k, k, k, v, v, v, v)


def _halo_specs(tm, width, total_rows):
    per = tm // HALO_ROWS
    last = total_rows // HALO_ROWS - 1
    prev = pl.BlockSpec((HALO_ROWS, width), lambda i, *_: (jnp.maximum(i * per - 1, 0), 0))
    nxt = pl.BlockSpec((HALO_ROWS, width), lambda i, *_: (jnp.minimum((i + 1) * per, last), 0))
    return prev, nxt


def _outproj_even_kernel(*refs, dims, alpha, n_streams):
    x_refs, refs = refs[:n_streams], refs[n_streams:]
    a_ref, z_ref, zp_ref, zn_ref, gb_ref, cw_ref, w_ref, mod_ref, g_ref, b_ref, o_ref, cat_ref = refs
    tm = o_ref.shape[0]
    row = lax.broadcasted_iota(I32, (tm, 1), 0)
    pos, length = _seq_position(pl.program_id(0) * tm + row, dims)
    z = z_ref[...].astype(F32)
    zprev = jnp.where(row == 0, zp_ref[HALO_ROWS - 1:HALO_ROWS, :].astype(F32), pltpu.roll(z, 1, 0))
    zprev = jnp.where(pos == 0, 0.0, zprev)
    znext = jnp.where(row == tm - 1, zn_ref[0:1, :].astype(F32), pltpu.roll(z, tm - 1, 0))
    znext = jnp.where(pos == length - 1, 0.0, znext)
    y = zprev * cw_ref[0:1, :] + z * cw_ref[1:2, :] + znext * cw_ref[2:3, :]
    cat_ref[:, 0:ATTN_WIDTH] = a_ref[...]
    cat_ref[:, ATTN_WIDTH:] = (gb_ref[...].astype(F32) * y).astype(BF16)
    mix = jnp.dot(cat_ref[...], w_ref[...], preferred_element_type=F32)
    o_ref[...] = _residual_ln(_stream_rows(x_refs, dims), mix, mod_ref[0, 2:3, :], g_ref[...], b_ref[...], alpha)


def _outproj_even(a, z, gb, conv_w, w_out, streams, mods, ln_g, ln_b, dims, rows, alpha, layer, j):
    d = dims.d
    conv_width = z.shape[1]
    tm = _pick_tile(512, dims.n, dims.r_ctx)
    zp_spec, zn_spec = _halo_specs(tm, conv_width, z.shape[0])
    return pl.pallas_call(
        functools.partial(_outproj_even_kernel, dims=dims, alpha=alpha, n_streams=len(streams)),
        grid=(rows // tm,),
        in_specs=_stream_specs(streams, tm, dims)
        + [_row_spec(tm, ATTN_WIDTH), _row_spec(tm, conv_width), zp_spec, zn_spec,
           _row_spec(tm, conv_width), _layer_spec(conv_w, j), _layer_spec(w_out, layer),
           _mod_spec(dims, tm, layer), _layer_spec(ln_g, 2 * layer), _layer_spec(ln_b, 2 * layer)],
        out_specs=_row_spec(tm, d),
        out_shape=jax.ShapeDtypeStruct((rows, d), F32),
        scratch_shapes=[pltpu.VMEM((tm, d), BF16)],
        compiler_params=_cparams("parallel"),
        name="outproj_even",
    )(*streams, a, z, z, z, gb, conv_w, w_out, mods, ln_g, ln_b)


def _ffn_kernel(*refs, alpha, n_cast):
    casts_in, refs = refs[:n_cast], refs[n_cast:]
    x_ref, mod_ref, wg_ref, wu_ref, wd_ref, g_ref, b_ref, o_ref = refs[:8]
    casts_out, (h_ref, acc_ref) = refs[8:8 + n_cast], refs[8 + n_cast:]
    for src_ref, dst_ref in zip(casts_in, casts_out):
        dst_ref[...] = src_ref[...].astype(BF16)
    k = pl.program_id(1)

    @pl.when(k == 0)
    def _():
        h_ref[...] = (x_ref[...] * (1.0 + mod_ref[0, 4:5, :]) + mod_ref[0, 3:4, :]).astype(BF16)
        acc_ref[...] = jnp.zeros_like(acc_ref)

    h = h_ref[...]
    gate = jnp.dot(h, wg_ref[...], preferred_element_type=F32)
    up = jnp.dot(h, wu_ref[...], preferred_element_type=F32)
    act = (_silu(gate) * up).astype(BF16)
    acc_ref[...] += jnp.dot(act, wd_ref[...], preferred_element_type=F32)

    @pl.when(k == pl.num_programs(1) - 1)
    def _():
        o_ref[...] = _residual_ln(x_ref[...], acc_ref[...], mod_ref[0, 5:6, :], g_ref[...], b_ref[...], alpha)


def _ffn(xs, mods, wg, wu, wd, ln_g, ln_b, dims, rows, alpha, layer, j, tf, experts=None):
    _, d, dff = wg.shape
    nk = dff // tf
    tm = _pick_tile(512, dims.n, dims.r_ctx)
    n_row = rows // tm
    in_specs = [pl.BlockSpec((tm, d), lambda i, k: (i, 0)), _mod_spec(dims, tm, layer),
                pl.BlockSpec((None, d, tf), lambda i, k: (j, 0, k)),
                pl.BlockSpec((None, d, tf), lambda i, k: (j, 0, k)),
                pl.BlockSpec((None, tf, d), lambda i, k: (j, k, 0)),
                _layer_spec(ln_g, 2 * layer + 1), _layer_spec(ln_b, 2 * layer + 1)]
    out_specs = [pl.BlockSpec((tm, d), lambda i, k: (i, 0))]
    out_shape = [jax.ShapeDtypeStruct((rows, d), F32)]
    cast_in, cast_specs = [], []
    if experts is not None:
        (eg, eu, ed), jj = experts
        _, n_exp, _, dff = eg.shape
        assert dff == nk * tf
        rb = 16
        while n_row * rb < n_exp * d:
            rb *= 2
        nrb = n_exp * d // rb
        rbd = n_exp * tf // nrb
        assert n_exp * d % rb == 0 and n_exp * tf % nrb == 0 and rbd % 16 == 0
        gu_in = pl.BlockSpec((None, rb, tf), lambda i, k: (jj, jnp.minimum(i, nrb - 1), k))
        dn_in = pl.BlockSpec((None, rbd, d), lambda i, k: (jj, jnp.minimum(i, nrb - 1) * nk + k, 0))
        cast_in = [eg.reshape(-1, n_exp * d, dff), eu.reshape(-1, n_exp * d, dff), ed.reshape(-1, n_exp * dff, d)]
        cast_specs = [gu_in, gu_in, dn_in]
        gu_out = pl.BlockSpec((None, rb, tf), lambda i, k: (k, i, 0))
        out_specs += [gu_out, gu_out, pl.BlockSpec((rbd, d), lambda i, k: (i * nk + k, 0))]
        out_shape += [jax.ShapeDtypeStruct((nk, n_row * rb, tf), BF16)] * 2
        out_shape += [jax.ShapeDtypeStruct((n_row * nk * rbd, d), BF16)]
    res = pl.pallas_call(
        functools.partial(_ffn_kernel, alpha=alpha, n_cast=len(cast_in)),
        grid=(n_row, nk),
        in_specs=cast_specs + in_specs,
        out_specs=out_specs,
        out_shape=out_shape,
        scratch_shapes=[pltpu.VMEM((tm, d), BF16), pltpu.VMEM((tm, d), F32)],
        compiler_params=_cparams("arbitrary", "arbitrary"),
        name="dense_ffn",
    )(*cast_in, xs, mods, wg, wu, wd, ln_g, ln_b)
    return res[0], (tuple(res[1:]) if experts is not None else None)


def _inproj_odd_kernel(x_ref, mod_ref, w_ref, up_ref, uf_ref):
    h = (x_ref[...] * (1.0 + mod_ref[0, 1:2, :]) + mod_ref[0, 0:1, :]).astype(BF16)
    pw = up_ref.shape[1]
    up_ref[...] = jnp.dot(h, w_ref[:, 0:pw], preferred_element_type=F32).astype(BF16)
    uf_ref[...] = jnp.dot(h, w_ref[:, pw:], preferred_element_type=F32).astype(BF16)


def _inproj_odd(xs, mods, w_in, dims, rows, pool_width, layer, j):
    d = dims.d
    four_width = w_in.shape[2] - pool_width
    tm = _pick_tile(512, dims.n, dims.r_ctx)
    return pl.pallas_call(
        _inproj_odd_kernel,
        grid=(rows // tm,),
        in_specs=[_row_spec(tm, d), _mod_spec(dims, tm, layer), _layer_spec(w_in, j)],
        out_specs=[_row_spec(tm, pool_width), _row_spec(tm, four_width)],
        out_shape=[jax.ShapeDtypeStruct((rows, pool_width), BF16), jax.ShapeDtypeStruct((rows, four_width), BF16)],
        compiler_params=_cparams("parallel"),
        name="inproj_odd",
    )(xs, mods, w_in)


def _seq_dft_constants(n, group):
    n2 = n // FFT_RADIX
    b = np.arange(n2, dtype=np.float64)
    ka = np.arange(FFT_RADIX, dtype=np.float64)
    ang = 2.0 * np.pi * np.outer(b, ka) / n
    tw = np.concatenate([np.cos(ang), -np.sin(ang)], axis=1)
    ang2 = 2.0 * np.pi * np.outer(b, b) / n2
    cs, sn = np.cos(ang2), np.sin(ang2)
    mat = np.block([[cs, sn], [-sn, cs]]) / np.sqrt(float(n) * group)
    return jnp.asarray(tw, F32), jnp.asarray(mat, BF16)


def _seq_dft_kernel(x_ref, tw_ref, m_ref, or_ref, oi_ref):
    n2 = x_ref.shape[1]
    lanes = x_ref.shape[2]
    for ka in range(FFT_RADIX):
        yr = yi = None
        for a in range(FFT_RADIX):
            ang = 2.0 * np.pi * ((a * ka) % FFT_RADIX) / FFT_RADIX
            cr, ci = float(np.round(np.cos(ang), 12)), float(np.round(-np.sin(ang), 12))
            xa = x_ref[a].astype(F32)
            if cr != 0.0:
                yr = cr * xa if yr is None else yr + cr * xa
            if ci != 0.0:
                yi = ci * xa if yi is None else yi + ci * xa
        if ka == 0:
            zr, zi = yr, jnp.zeros_like(yr)
        else:
            twr = jnp.broadcast_to(tw_ref[:, ka:ka + 1], (n2, lanes))
            twi = jnp.broadcast_to(tw_ref[:, FFT_RADIX + ka:FFT_RADIX + ka + 1], (n2, lanes))
            if yi is None:
                zr, zi = yr * twr, yr * twi
            else:
                zr, zi = yr * twr - yi * twi, yr * twi + yi * twr
        zcat = jnp.concatenate([zr, zi], axis=0).astype(BF16)
        res = jnp.dot(m_ref[...], zcat, preferred_element_type=F32)
        or_ref[ka] = res[0:n2].astype(BF16)
        oi_ref[ka] = res[n2:].astype(BF16)


def _seq_dft(uf, row0, bsz, n, group):
    rows, width = uf.shape
    n2 = n // FFT_RADIX
    lanes = _pick_tile(FFT_LANES, width)
    tw, mat = _seq_dft_constants(n, group)
    x3 = uf.reshape(rows // n2, n2, width)
    slab0 = row0 // n
    out = jax.ShapeDtypeStruct((bsz, FFT_RADIX, n2, width), BF16)
    ospec = pl.BlockSpec((None, FFT_RADIX, n2, lanes), lambda b, c: (b, 0, 0, c))
    return pl.pallas_call(
        _seq_dft_kernel,
        grid=(bsz, width // lanes),
        in_specs=[pl.BlockSpec((FFT_RADIX, n2, lanes), lambda b, c: (slab0 + b, 0, c)),
                  pl.BlockSpec(tw.shape, lambda b, c: (0, 0)),
                  pl.BlockSpec(mat.shape, lambda b, c: (0, 0), pipeline_mode=pl.Buffered(1))],
        out_specs=[ospec, ospec],
        out_shape=[out, out],
        compiler_params=_cparams("parallel", "parallel"),
        name="seq_dft",
    )(x3, tw, mat)


def _top2_routing(logits, n_exp):
    lane = lax.broadcasted_iota(I32, logits.shape, 1)
    lg = jnp.where(lane < n_exp, logits, -jnp.inf)
    m1 = jnp.max(lg, axis=-1, keepdims=True)
    i1 = jnp.min(jnp.where(lg == m1, lane, ROUTE_LANES), axis=-1, keepdims=True)
    lg2 = jnp.where(lane == i1, -jnp.inf, lg)
    m2 = jnp.max(lg2, axis=-1, keepdims=True)
    i2 = jnp.min(jnp.where(lg2 == m2, lane, ROUTE_LANES), axis=-1, keepdims=True)
    e2 = jnp.exp(m2 - m1)
    g1 = 1.0 / (1.0 + e2)
    g2 = e2 * g1
    return jnp.where(lane == 0, i1.astype(F32),
                     jnp.where(lane == 1, i2.astype(F32), jnp.where(lane == 2, g1, jnp.where(lane == 3, g2, 0.0))))


def _outproj_odd_kernel(*refs, dims, alpha, has_ctx, n_exp):
    (up_ref, upp_ref, upn_ref, xr_ref, xi_ref) = refs[:5]
    refs = refs[5:]
    if has_ctx:
        (cr_ref, ci_ref) = refs[:2]
        refs = refs[2:]
    (perm_ref, cs_ref, fw_ref, pw_ref, ps_ref, w_ref, x_ref, mod_ref, g_ref, b_ref, rwh_ref, rwl_ref,
     o_ref, h2_ref, route_ref, ext_ref, cat_ref) = refs
    tm = x_ref.shape[0]
    i = pl.program_id(0)
    row = lax.broadcasted_iota(I32, (tm, 1), 0)
    pos, length = _seq_position(i * tm + row, dims)
    first_pos, _ = _seq_position(i * tm, dims)
    last_pos, last_len = _seq_position(i * tm + tm - 1, dims)

    pool_width = up_ref.shape[1]
    group = pool_width // len(POOL_WINDOWS)
    h0 = POOL_HALO
    ext_ref[0:h0, :] = jnp.where(first_pos == 0, 0.0, upp_ref[HALO_ROWS - h0:HALO_ROWS, :].astype(F32))
    ext_ref[h0:h0 + tm, :] = up_ref[...].astype(F32)
    ext_ref[h0 + tm:h0 + tm + h0, :] = jnp.where(last_pos == last_len - 1, 0.0, upn_ref[0:h0, :].astype(F32))
    for gi, w in enumerate(POOL_WINDOWS):
        cols = slice(gi * group, (gi + 1) * group)
        back, ahead = w // 2, w - w // 2 - 1
        tot = None
        for s in range(-back, ahead + 1):
            t = ext_ref[h0 + s:h0 + s + tm, cols]
            tot = t if tot is None else tot + t
        cnt = jnp.minimum(pos + ahead, length - 1) - jnp.maximum(pos - back, 0) + 1
        pooled = tot / cnt.astype(F32) - ext_ref[h0:h0 + tm, cols]
        yp = jnp.dot(pooled.astype(BF16), pw_ref[gi], preferred_element_type=F32) * ps_ref[:, cols]
        cat_ref[:, cols] = yp.astype(BF16)

    if has_ctx:
        lat = i * tm < dims.r_lat
        xr = jnp.where(lat, xr_ref[...], cr_ref[...])
        xi = jnp.where(lat, xi_ref[...], ci_ref[...])
    else:
        xr, xi = xr_ref[...], xi_ref[...]
    four_width = xr.shape[-1]
    fgroup = four_width // FOURIER_HEADS
    xr = jnp.dot(perm_ref[...], xr.reshape(tm, four_width), preferred_element_type=F32).astype(BF16)
    xi = jnp.dot(perm_ref[...], xi.reshape(tm, four_width), preferred_element_type=F32).astype(BF16)
    for hd in range(FOURIER_HEADS):
        cols = slice(hd * fgroup, (hd + 1) * fgroup)
        both = jnp.concatenate([xr[:, cols], xi[:, cols]], axis=1)
        f = jnp.dot(both, cs_ref[...], preferred_element_type=F32)
        yf = jnp.dot(f.astype(BF16), fw_ref[hd], preferred_element_type=F32)
        cat_ref[:, pool_width + hd * fgroup:pool_width + (hd + 1) * fgroup] = yf.astype(BF16)

    mix = jnp.dot(cat_ref[...], w_ref[...], preferred_element_type=F32)
    xn = _residual_ln(x_ref[...], mix, mod_ref[0, 2:3, :], g_ref[...], b_ref[...], alpha)
    o_ref[...] = xn
    h2 = xn * (1.0 + mod_ref[0, 4:5, :]) + mod_ref[0, 3:4, :]
    h2_ref[...] = h2
    h_hi = h2.astype(BF16)
    h_lo = (h2 - h_hi.astype(F32)).astype(BF16)
    logits = (jnp.dot(h_hi, rwh_ref[...], preferred_element_type=F32)
              + (jnp.dot(h_lo, rwh_ref[...], preferred_element_type=F32)
                 + jnp.dot(h_hi, rwl_ref[...], preferred_element_type=F32)))
    route_ref[...] = _top2_routing(logits, n_exp)


def _channel_dft_matrix(group):
    c = np.arange(group, dtype=np.float64)
    ang = 2.0 * np.pi * np.outer(c, c) / group
    return jnp.asarray(np.concatenate([np.cos(ang), np.sin(ang)], axis=0), BF16)


def _tile_permutation(tm):
    per = tm // FFT_RADIX
    p = np.zeros((tm, tm), np.float32)
    for ka in range(FFT_RADIX):
        for kb in range(per):
            p[kb * FFT_RADIX + ka, ka * per + kb] = 1.0
    return jnp.asarray(p, BF16)


def _outproj_odd(up, lat_ri, ctx_ri, fourier_w, pool_w, pool_scale, w_out, xs, mods, ln_g, ln_b, rw_hi, rw_lo,
                 n_exp, dims, rows, alpha, layer, j):
    d = dims.d
    tm = SEQ_TILE
    pool_width = up.shape[1]
    four_width = lat_ri[0].shape[-1]
    fgroup = four_width // FOURIER_HEADS
    has_ctx = ctx_ri is not None
    per = tm // FFT_RADIX
    lat_tiles = dims.n // tm
    n_lat = dims.r_lat // tm
    upp_spec, upn_spec = _halo_specs(tm, pool_width, up.shape[0])

    def lat_map(i):
        t = jnp.minimum(i, n_lat - 1)
        return (t // lat_tiles, 0, t % lat_tiles, 0)

    lat_spec = pl.BlockSpec((None, FFT_RADIX, per, four_width), lat_map)
    operands = [up, up, up, lat_ri[0], lat_ri[1]]
    specs = [_row_spec(tm, pool_width), upp_spec, upn_spec, lat_spec, lat_spec]
    if has_ctx:
        ctx_tiles = dims.n_ctx // tm

        def ctx_map(i):
            t = jnp.maximum(i - n_lat, 0)
            return (t // ctx_tiles, 0, t % ctx_tiles, 0)

        ctx_spec = pl.BlockSpec((None, FFT_RADIX, per, four_width), ctx_map)
        operands += [ctx_ri[0], ctx_ri[1]]
        specs += [ctx_spec, ctx_spec]
    perm, cs = _tile_permutation(tm), _channel_dft_matrix(fgroup)
    operands += [perm, cs, fourier_w, pool_w, pool_scale, w_out, xs, mods, ln_g, ln_b, rw_hi, rw_lo]
    specs += [_const_spec(perm.shape), _const_spec(cs.shape), _layer_spec(fourier_w, j), _layer_spec(pool_w, j),
              _layer_spec(pool_scale, j), _layer_spec(w_out, layer), _row_spec(tm, d),
              _mod_spec(dims, tm, layer), _layer_spec(ln_g, 2 * layer), _layer_spec(ln_b, 2 * layer),
              _layer_spec(rw_hi, j), _layer_spec(rw_lo, j)]
    return pl.pallas_call(
        functools.partial(_outproj_odd_kernel, dims=dims, alpha=alpha, has_ctx=has_ctx, n_exp=n_exp),
        grid=(rows // tm,),
        in_specs=specs,
        out_specs=[_row_spec(tm, d), _row_spec(tm, d), _row_spec(tm, ROUTE_LANES)],
        out_shape=[jax.ShapeDtypeStruct((rows, d), F32), jax.ShapeDtypeStruct((rows, d), F32),
                   jax.ShapeDtypeStruct((rows, ROUTE_LANES), F32)],
        scratch_shapes=[pltpu.VMEM((tm + 2 * POOL_HALO, pool_width), F32), pltpu.VMEM((tm, d), BF16)],
        compiler_params=_cparams("parallel"),
        name="outproj_odd",
    )(*operands)


def _dispatch_plan(route, n_exp, tile):
    tokens = route.shape[0]
    assign = TOP_K * tokens
    experts = route[:, 0:TOP_K].astype(I32).reshape(assign)
    onehot = (experts[:, None] == jnp.arange(n_exp, dtype=I32)[None, :]).astype(I32)
    running = jnp.cumsum(onehot, axis=0)
    rank = jnp.sum((running - onehot) * onehot, axis=1)
    counts = running[-1]
    padded = ((counts + tile - 1) // tile) * tile
    ends = jnp.cumsum(padded)
    starts = ends - padded
    slot = jnp.sum(onehot * starts[None, :], axis=1) + rank
    n_tiles = -(-(assign + n_exp * (tile - 1)) // tile)
    tile_start = jnp.arange(n_tiles, dtype=I32) * tile
    tile_expert = jnp.minimum(jnp.sum((tile_start[:, None] >= ends[None, :]).astype(I32), axis=1), n_exp - 1)
    live_tiles = ends[-1] // tile
    tile_expert = jnp.where(tile_start < ends[-1], tile_expert, tile_expert[jnp.maximum(live_tiles - 1, 0)])
    pad_first = starts + counts
    pad_head = jnp.minimum((-pad_first) % SUBLANES, padded - counts)
    pad_info = jnp.concatenate([pad_first, pad_head, pad_first + pad_head, padded - counts - pad_head,
                                live_tiles[None]]).astype(I32)
    return slot, pad_info, tile_expert, n_tiles


def _dispatch_kernel(slot_ref, pad_ref, h_ref, o_ref, zero_ref, sem, zsem, *, n_exp, tile, n_tiles, min_tiles):
    i = pl.program_id(0)
    tm = h_ref.shape[0]

    def zero_copy(first_row, size):
        return pltpu.make_async_copy(zero_ref.at[pl.ds(0, size)], o_ref.at[pl.ds(first_row, size)], zsem)

    def padding_copies(act):
        for e in range(n_exp):
            first, head = pad_ref[e], pad_ref[n_exp + e]
            for r in range(SUBLANES - 1):
                pl.when(r < head)(functools.partial(act, zero_copy(first + r, 1)))
            first, length = pad_ref[2 * n_exp + e], pad_ref[3 * n_exp + e]
            for bit in reversed(range(SUBLANES.bit_length() - 1, tile.bit_length() - 1)):
                size = 1 << bit
                for part in range(-(-size // ZERO_ROWS)):
                    rows = min(size, ZERO_ROWS)
                    at = pl.multiple_of(first + part * rows, SUBLANES)
                    pl.when((length & size) != 0)(functools.partial(act, zero_copy(at, rows)))
                first = first + (length & size)
        live = pad_ref[PAD_LIVE * n_exp]
        for t in range(min_tiles, n_tiles):
            for part in range(tile // ZERO_ROWS):
                pl.when(t >= live)(functools.partial(act, zero_copy(t * tile + part * ZERO_ROWS, ZERO_ROWS)))

    @pl.when(i == 0)
    def _():
        zero_ref[...] = jnp.zeros_like(zero_ref)
        padding_copies(lambda cp: cp.start())
        padding_copies(lambda cp: cp.wait())

    base = i * tm * TOP_K

    def issue(r, carry):
        for c in range(TOP_K):
            dst = slot_ref[base + r * TOP_K + c]
            pltpu.make_async_copy(h_ref.at[pl.ds(r, 1)], o_ref.at[pl.ds(dst, 1)], sem).start(priority=c % 2)
        return carry

    lax.fori_loop(0, tm, issue, 0, unroll=ISSUE_UNROLL)
    for c in range(TOP_K):
        pltpu.make_async_copy(h_ref, o_ref.at[pl.ds(0, tm)], sem).wait()


def _dispatch_rows(slot, pad_info, h2, n_exp, tile, n_tiles):
    tokens, d = h2.shape
    tm = SEQ_TILE
    min_tiles = -(-(TOP_K * tokens) // tile)
    return pl.pallas_call(
        functools.partial(_dispatch_kernel, n_exp=n_exp, tile=tile, n_tiles=n_tiles, min_tiles=min_tiles),
        grid_spec=pltpu.PrefetchScalarGridSpec(
            num_scalar_prefetch=2,
            grid=(tokens // tm,),
            in_specs=[_row_spec(tm, d)],
            out_specs=pl.BlockSpec(memory_space=pl.ANY),
            scratch_shapes=[pltpu.VMEM((ZERO_ROWS, d), h2.dtype), pltpu.SemaphoreType.DMA(()),
                            pltpu.SemaphoreType.DMA(())]),
        out_shape=jax.ShapeDtypeStruct((n_tiles * tile, d), h2.dtype),
        compiler_params=_cparams("arbitrary", unchecked=True),
        name="moe_dispatch",
    )(slot, pad_info, h2)


def _expert_kernel(te_ref, pad_ref, x_ref, wg_ref, wu_ref, wd_ref, o_ref, wgb_ref, wub_ref, wdb_ref, h_ref, acc_ref,
                   sem, *, n_exp, nk):
    i = pl.program_id(0)
    live_tiles = pad_ref[PAD_LIVE * n_exp]
    d = x_ref.shape[1]
    tf = wgb_ref.shape[2]

    def chunk_copies(expert, c, slot):
        rows = pl.ds(pl.multiple_of(expert * d, d), d)
        down = pl.ds(pl.multiple_of((expert * nk + c) * tf, tf), tf)
        return (pltpu.make_async_copy(wg_ref.at[c, rows, :], wgb_ref.at[slot], sem.at[0, slot]),
                pltpu.make_async_copy(wu_ref.at[c, rows, :], wub_ref.at[slot], sem.at[1, slot]),
                pltpu.make_async_copy(wd_ref.at[down, :], wdb_ref.at[slot], sem.at[2, slot]))

    @pl.when(i < live_tiles)
    def _():
        expert = te_ref[i]

        @pl.when(i == 0)
        def _():
            for cp in chunk_copies(expert, 0, 0):
                cp.start()

        h_ref[...] = x_ref[...].astype(BF16)
        for c in range(nk):
            slot = c % WEIGHT_SLOTS
            for cp in chunk_copies(expert, c, slot):
                cp.wait()
            if c + 1 < nk:
                for cp in chunk_copies(expert, c + 1, (c + 1) % WEIGHT_SLOTS):
                    cp.start()
            else:
                @pl.when(i + 1 < live_tiles)
                def _():
                    for cp in chunk_copies(te_ref[i + 1], 0, 0):
                        cp.start()
            h = h_ref[...]
            gate = jnp.dot(h, wgb_ref[slot], preferred_element_type=F32)
            up = jnp.dot(h, wub_ref[slot], preferred_element_type=F32)
            act = (_silu(gate) * up).astype(BF16)
            part = jnp.dot(act, wdb_ref[slot], preferred_element_type=F32)
            if c == 0:
                acc_ref[...] = part
            else:
                acc_ref[...] += part
        o_ref[...] = acc_ref[...]

    @pl.when(i >= live_tiles)
    def _():
        o_ref[...] = jnp.zeros_like(o_ref)


def _expert_ffn(tile_expert, pad_info, xs, wg, wu, wd, n_exp):
    n_slots, d = xs.shape
    nk, _, tf = wg.shape
    tm = EXPERT_TILE
    assert (nk - 1) % WEIGHT_SLOTS != 0
    return pl.pallas_call(
        functools.partial(_expert_kernel, n_exp=n_exp, nk=nk),
        grid_spec=pltpu.PrefetchScalarGridSpec(
            num_scalar_prefetch=2,
            grid=(n_slots // tm,),
            in_specs=[pl.BlockSpec((tm, d), lambda i, te, pad: (i, 0)),
                      pl.BlockSpec(memory_space=pl.ANY), pl.BlockSpec(memory_space=pl.ANY),
                      pl.BlockSpec(memory_space=pl.ANY)],
            out_specs=pl.BlockSpec((tm, d), lambda i, te, pad: (i, 0)),
            scratch_shapes=[pltpu.VMEM((WEIGHT_SLOTS, d, tf), BF16), pltpu.VMEM((WEIGHT_SLOTS, d, tf), BF16),
                            pltpu.VMEM((WEIGHT_SLOTS, tf, d), BF16), pltpu.VMEM((tm, d), BF16),
                            pltpu.VMEM((tm, d), F32), pltpu.SemaphoreType.DMA((3, WEIGHT_SLOTS))]),
        out_shape=jax.ShapeDtypeStruct((n_slots, d), F32),
        compiler_params=_cparams("arbitrary"),
        name="expert_ffn",
    )(tile_expert, pad_info, xs, wg, wu, wd)


def _combine_kernel(slot_ref, y_ref, route_ref, x_ref, mod_ref, g_ref, b_ref, o_ref, buf_ref, sem, *, alpha):
    tm = x_ref.shape[0]
    i = pl.program_id(0)

    def fetch(tile, half):
        base = tile * tm * TOP_K

        def issue(r, carry):
            for c in range(TOP_K):
                src = slot_ref[base + r * TOP_K + c]
                pltpu.make_async_copy(y_ref.at[pl.ds(src, 1)], buf_ref.at[half, c, pl.ds(r, 1)],
                                      sem.at[half]).start(priority=c % 2)
            return carry

        lax.fori_loop(0, tm, issue, 0, unroll=ISSUE_UNROLL)

    @pl.when(i == 0)
    def _():
        fetch(0, 0)

    @pl.when(i + 1 < pl.num_programs(0))
    def _():
        fetch(i + 1, (i + 1) % 2)

    half = i % 2
    for c in range(TOP_K):
        pltpu.make_async_copy(y_ref.at[pl.ds(0, tm)], buf_ref.at[half, c], sem.at[half]).wait()
    route = route_ref[...]
    ff = route[:, 2:3] * buf_ref[half, 0] + route[:, 3:4] * buf_ref[half, 1]
    o_ref[...] = _residual_ln(x_ref[...], ff, mod_ref[0, 5:6, :], g_ref[...], b_ref[...], alpha)


def _combine(slot, ys, route, xs, mods, ln_g, ln_b, dims, rows, alpha, layer):
    d = dims.d
    tm = SEQ_TILE
    return pl.pallas_call(
        functools.partial(_combine_kernel, alpha=alpha),
        grid_spec=pltpu.PrefetchScalarGridSpec(
            num_scalar_prefetch=1,
            grid=(rows // tm,),
            in_specs=[pl.BlockSpec(memory_space=pl.ANY), _row_spec(tm, ROUTE_LANES), _row_spec(tm, d),
                      _mod_spec(dims, tm, layer), _layer_spec(ln_g, 2 * layer + 1),
                      _layer_spec(ln_b, 2 * layer + 1)],
            out_specs=_row_spec(tm, d),
            scratch_shapes=[pltpu.VMEM((2, TOP_K, tm, d), F32), pltpu.SemaphoreType.DMA((2,))]),
        out_shape=jax.ShapeDtypeStruct((rows, d), F32),
        compiler_params=_cparams("arbitrary", unchecked=True),
        name="moe_combine",
    )(slot, ys, route, xs, mods, ln_g, ln_b)


def _moe(xs, h2, route, mods, wg, wu, wd, n_exp, ln_g, ln_b, dims, rows, alpha, layer):
    slot, pad_info, tile_expert, n_tiles = _dispatch_plan(route, n_exp, EXPERT_TILE)
    sorted_rows = _dispatch_rows(slot, pad_info, h2, n_exp, EXPERT_TILE, n_tiles)
    ys = _expert_ffn(tile_expert, pad_info, sorted_rows, wg, wu, wd, n_exp)
    return _combine(slot, ys, route, xs, mods, ln_g, ln_b, dims, rows, alpha, layer)


def kernel(x, c, ctx, c_ctx, w_mod, b_mod, w_mix_out, ln_g, ln_b, w_in_ab, conv_w, attn_sink, w_in_cd, pool_w,
           pool_scale, fourier_w, ffn_w_gate, ffn_w_up, ffn_w_down, router_w, moe_w_gate, moe_w_up, moe_w_down):
    bsz, n, d = x.shape
    n_ctx = ctx.shape[1]
    depth = w_mod.shape[0]
    n_exp = router_w.shape[2]
    dims = Dims(bsz, n, n_ctx, d)
    assert n % SEQ_TILE == 0 and n_ctx % SEQ_TILE == 0 and n % GRID_W == 0
    assert dims.r_lat % (FFT_RADIX * (n_ctx // FFT_RADIX)) == 0 and dims.r_all % (n // FFT_RADIX) == 0
    alpha = (2.0 * depth) ** 0.25
    pool_width = pool_w.shape[1] * pool_w.shape[2]
    fgroup = fourier_w.shape[2]

    mods = _modulations(c, c_ctx, w_mod, b_mod)
    w_out = w_mix_out.astype(BF16)
    w_ab, w_cd = w_in_ab.astype(BF16), w_in_cd.astype(BF16)
    tf = _pick_tile(512, ffn_w_gate.shape[2], moe_w_gate.shape[3])
    ffn_g, ffn_u, ffn_d = ffn_w_gate.astype(BF16), ffn_w_up.astype(BF16), ffn_w_down.astype(BF16)
    four_w, pool_wb = fourier_w.astype(BF16), pool_w.astype(BF16)
    pool_sc = pool_scale[:, None, :]
    lng, lnb = ln_g.reshape(2 * depth, 1, d), ln_b.reshape(2 * depth, 1, d)
    rw = jnp.zeros(router_w.shape[:2] + (ROUTE_LANES,), F32).at[:, :, :n_exp].set(router_w)
    rw_hi = rw.astype(BF16)
    rw_lo = (rw - rw_hi.astype(F32)).astype(BF16)

    streams = [x.reshape(dims.r_lat, d), ctx.reshape(dims.r_ctx, d)]
    expert_bf = None
    for l in range(depth):
        even = l % 2 == 0
        j = l // 2
        ctx_out = any(m % 2 == 0 for m in range(l + 1, depth))
        rows = dims.r_all if ctx_out else dims.r_lat
        if even:
            q, k, v, z, gb = _inproj_even(streams, mods, w_ab, dims, l, j)
            a = _attention(q, k, v, attn_sink, dims, ctx_out, j)
            xs = _outproj_even(a, z, gb, conv_w, w_out, streams, mods, lng, lnb, dims, rows, alpha, l, j)
            experts = ((moe_w_gate, moe_w_up, moe_w_down), j) if l + 1 < depth else None
            xs, expert_bf = _ffn(xs, mods, ffn_g, ffn_u, ffn_d, lng, lnb, dims, rows, alpha, l, j, tf, experts)
        else:
            xs = streams[0]
            up, uf = _inproj_odd(xs, mods, w_cd, dims, rows, pool_width, l, j)
            lat_ri = _seq_dft(uf, 0, bsz, n, fgroup)
            ctx_ri = _seq_dft(uf, dims.r_lat, bsz, n_ctx, fgroup) if ctx_out else None
            xs, h2, route = _outproj_odd(up, lat_ri, ctx_ri, four_w, pool_wb, pool_sc, w_out, xs, mods, lng, lnb,
                                         rw_hi, rw_lo, n_exp, dims, rows, alpha, l, j)
            xs = _moe(xs, h2, route, mods, *expert_bf, n_exp, lng, lnb, dims, rows, alpha, l)
        streams = [xs]
    return xs[:dims.r_lat].reshape(bsz, n, d)
```
